```python
import math
import jax, jax.numpy as jnp
from jax import lax
import numpy as np


D_MODEL = 4096
BATCH = 4
SEQ = 4096
DEPTH = 1

HEAD_DIM = 128
N_MOBA_HEADS = 16
N_SWA_HEADS = 16
N_SWA_KV_HEADS = 4
MOBA_BLOCK = 256
MOBA_TOPK = 3
MOBA_QCHUNK = 64
SWA_WINDOW = 128
SWA_BLOCK = 128
N_MEM = 256
N_CROSS_HEADS = 4
D_FF = 11008
NUM_BUCKETS = 32
MAX_DISTANCE = 128
EPS = 1e-6

MOBA_W = N_MOBA_HEADS * HEAD_DIM
SWA_W = N_SWA_HEADS * HEAD_DIM
SWA_KV_W = N_SWA_KV_HEADS * HEAD_DIM
CROSS_W = N_CROSS_HEADS * HEAD_DIM
MIX_W = MOBA_W + SWA_W
IN_W = 3 * MOBA_W + SWA_W + 2 * SWA_KV_W

kernel_name = 'hymba_moba_swa_sink_macaron_t5'


def rms_norm(x, g):
    xf = x.astype(jnp.float32)
    y = xf * lax.rsqrt(jnp.mean(xf * xf, axis=-1, keepdims=True) + EPS)
    return (y * g.astype(jnp.float32)).astype(x.dtype)


def swiglu(x, w_gate, w_up, w_down):
    return (jax.nn.silu(x @ w_gate) * (x @ w_up)) @ w_down


def t5_bucket(dist):
    n = jnp.maximum(dist, 0)
    max_exact = NUM_BUCKETS // 2
    nf = jnp.maximum(n, max_exact).astype(jnp.float32)
    large = max_exact + (jnp.log(nf / max_exact) / math.log(MAX_DISTANCE / max_exact)
                         * (NUM_BUCKETS - max_exact)).astype(jnp.int32)
    large = jnp.minimum(large, NUM_BUCKETS - 1)
    return jnp.where(n < max_exact, n, large)


def moba_attention(q, k, v, bias_hb):
    b_, s_, h_, d_ = q.shape
    L = MOBA_BLOCK
    nb = -(-s_ // L)
    topk = min(MOBA_TOPK, nb)
    pad = nb * L - s_

    def to_blocks(t):
        t = jnp.pad(t, ((0, 0), (0, pad), (0, 0), (0, 0)))
        return t.reshape(b_, nb, L, h_, d_).transpose(0, 3, 1, 2, 4)

    k_blk = to_blocks(k)
    v_blk = to_blocks(v)
    k_mean = jnp.mean(k_blk.astype(jnp.float32), axis=3)
    n_chunks = s_ // MOBA_QCHUNK
    q_c = q.reshape(b_, n_chunks, MOBA_QCHUNK, h_, d_).transpose(0, 1, 3, 2, 4)
    q_c = q_c.reshape(b_ * n_chunks, h_, MOBA_QCHUNK, d_)
    b_idx = jnp.repeat(jnp.arange(b_, dtype=jnp.int32), n_chunks)
    c_idx = jnp.tile(jnp.arange(n_chunks, dtype=jnp.int32), b_)
    scale = d_ ** -0.5
    blk_ids = jnp.arange(nb, dtype=jnp.int32)
    t_ids = jnp.arange(L, dtype=jnp.int32)
    gather = jax.vmap(lambda tb, ib: tb[ib])

    def one_chunk(args):
        qc, bi, ci = args
        kb, vb, km = k_blk[bi], v_blk[bi], k_mean[bi]
        q_pos = ci * MOBA_QCHUNK + jnp.arange(MOBA_QCHUNK, dtype=jnp.int32)
        own = (ci * MOBA_QCHUNK) // L
        gate = jnp.einsum('hqd,hnd->hqn', qc.astype(jnp.float32), km)
        gate = jnp.where(blk_ids < own, gate, -jnp.inf)
        _, sel = lax.top_k(gate, topk)
        valid = sel < own
        k_sel = gather(kb, sel)
        v_sel = gather(vb, sel)
        k_pos_sel = sel[..., None] * L + t_ids
        bkt_sel = t5_bucket(q_pos[None, :, None, None] - k_pos_sel).reshape(h_, -1)
        b_sel = jnp.take_along_axis(bias_hb, bkt_sel, axis=1).reshape(k_pos_sel.shape)
        s_sel = jnp.einsum('hqd,hqkld->hqkl', qc, k_sel).astype(jnp.float32) * scale + b_sel
        s_sel = jnp.where(valid[..., None], s_sel, -jnp.inf)
        k_own = lax.dynamic_index_in_dim(kb, own, axis=1, keepdims=False)
        v_own = lax.dynamic_index_in_dim(vb, own, axis=1, keepdims=False)
        d_own = q_pos[:, None] - (own * L + t_ids)[None, :]
        s_own = jnp.einsum('hqd,hld->hql', qc, k_own).astype(jnp.float32) * scale + bias_hb[:, t5_bucket(d_own)]
        s_own = jnp.where(d_own >= 0, s_own, -jnp.inf)
        logits = jnp.concatenate([s_sel.reshape(h_, MOBA_QCHUNK, topk * L), s_own], axis=-1)
        p = jax.nn.softmax(logits, axis=-1).astype(v.dtype)
        p_sel = p[..., :topk * L].reshape(h_, MOBA_QCHUNK, topk, L)
        p_own = p[..., topk * L:]
        return jnp.einsum('hqkl,hqkld->hqd', p_sel, v_sel) + jnp.einsum('hql,hld->hqd', p_own, v_own)

    out = lax.map(one_chunk, (q_c, b_idx, c_idx))
    out = out.reshape(b_, n_chunks, h_, MOBA_QCHUNK, d_).transpose(0, 1, 3, 2, 4)
    return out.reshape(b_, s_, h_, d_)


def swa_sink_attention(q, k, v, sinks, bias_hb):
    b_, s_, hq, d_ = q.shape
    hkv = k.shape[2]
    g_ = hq // hkv
    L = SWA_BLOCK
    nq = s_ // L
    qb = q.reshape(b_, nq, L, hkv, g_, d_)

    def band(t):
        prev = jnp.pad(t, ((0, 0), (L, 0), (0, 0), (0, 0)))[:, :s_]
        return jnp.concatenate([prev.reshape(b_, nq, L, hkv, d_), t.reshape(b_, nq, L, hkv, d_)], axis=2)

    kb, vb = band(k), band(v)
    scale = d_ ** -0.5
    i = jnp.arange(L, dtype=jnp.int32)[:, None]
    j = jnp.arange(2 * L, dtype=jnp.int32)[None, :]
    dist = i + L - j
    bias = bias_hb[:, t5_bucket(dist)].reshape(hkv, g_, 1, L, 2 * L)
    k_pos = (jnp.arange(nq, dtype=jnp.int32)[:, None, None] - 1) * L + j[None]
    mask = (dist >= 0) & (dist < SWA_WINDOW) & (k_pos >= 0)
    s = jnp.einsum('bnqhgd,bnkhd->bhgnqk', qb, kb).astype(jnp.float32) * scale + bias
    s = jnp.where(mask, s, -jnp.inf)
    sink = jnp.broadcast_to(sinks.astype(jnp.float32).reshape(1, hkv, g_, 1, 1, 1), s.shape[:-1] + (1,))
    p = jax.nn.softmax(jnp.concatenate([s, sink], axis=-1), axis=-1)[..., :-1].astype(v.dtype)
    o = jnp.einsum('bhgnqk,bnkhd->bnqhgd', p, vb)
    return o.reshape(b_, s_, hq, d_)


def cross_attention(n, mem_n, w_q, w_k, w_v, w_o):
    b_, s_, _ = n.shape
    m_ = mem_n.shape[1]
    q = (n @ w_q).reshape(b_, s_, N_CROSS_HEADS, HEAD_DIM)
    k = (mem_n @ w_k).reshape(b_, m_, N_CROSS_HEADS, HEAD_DIM)
    v = (mem_n @ w_v).reshape(b_, m_, N_CROSS_HEADS, HEAD_DIM)
    s = jnp.einsum('bshd,bmhd->bhsm', q, k).astype(jnp.float32) * (HEAD_DIM ** -0.5)
    p = jax.nn.softmax(s, axis=-1).astype(v.dtype)
    o = jnp.einsum('bhsm,bmhd->bshd', p, v).reshape(b_, s_, CROSS_W)
    return o @ w_o


def setup_inputs(seed: int = 0) -> dict:
    key = jax.random.key(seed)
    ks = jax.random.split(key, 32)
    f32 = jnp.float32
    L, D = DEPTH, D_MODEL

    def nrm(k, shape, scale):
        return jax.random.normal(k, shape, f32) * scale

    def gain(k, shape):
        return 1.0 + 0.02 * jax.random.normal(k, shape, f32)

    return {
        'x': nrm(ks[0], (BATCH, SEQ, D), 1.0),
        'mem': nrm(ks[1], (BATCH, N_MEM, D), 1.0),
        'rel_bias': nrm(ks[2], (NUM_BUCKETS, N_MOBA_HEADS + N_SWA_HEADS), 0.5),
        'g_final': gain(ks[3], (D,)),
        'g_ffn1': gain(ks[4], (L, D)),
        'w1_gate': nrm(ks[5], (L, D, D_FF), D ** -0.5),
        'w1_up': nrm(ks[6], (L, D, D_FF), D ** -0.5),
        'w1_down': nrm(ks[7], (L, D_FF, D), D_FF ** -0.5),
        'g_mix': gain(ks[8], (L, D)),
        'w_in': nrm(ks[9], (L, D, IN_W), D ** -0.5),
        'b_in': nrm(ks[10], (L, IN_W), 0.02),
        'sinks': nrm(ks[11], (L, N_SWA_HEADS), 0.5),
        'g_out_moba': gain(ks[12], (L, MOBA_W)),
        'g_out_swa': gain(ks[13], (L, SWA_W)),
        'w_out': nrm(ks[14], (L, MIX_W, D), MIX_W ** -0.5),
        'g_cross': gain(ks[15], (L, D)),
        'g_mem': gain(ks[16], (L, D)),
        'w_cq': nrm(ks[17], (L, D, CROSS_W), D ** -0.5),
        'w_ck': nrm(ks[18], (L, D, CROSS_W), D ** -0.5),
        'w_cv': nrm(ks[19], (L, D, CROSS_W), D ** -0.5),
        'w_co': nrm(ks[20], (L, CROSS_W, D), CROSS_W ** -0.5),
        'g_ffn2': gain(ks[21], (L, D)),
        'w2_gate': nrm(ks[22], (L, D, D_FF), D ** -0.5),
        'w2_up': nrm(ks[23], (L, D, D_FF), D ** -0.5),
        'w2_down': nrm(ks[24], (L, D_FF, D), D_FF ** -0.5),
    }


def reference(x, mem, rel_bias, g_final, g_ffn1, w1_gate, w1_up, w1_down, g_mix, w_in, b_in, sinks,
              g_out_moba, g_out_swa, w_out, g_cross, g_mem, w_cq, w_ck, w_cv, w_co,
              g_ffn2, w2_gate, w2_up, w2_down):
    b_, s_, _ = x.shape
    bias_moba = rel_bias[:, :N_MOBA_HEADS].T.astype(jnp.float32)
    bias_swa = rel_bias[:, N_MOBA_HEADS:].T.astype(jnp.float32)
    splits = [int(c) for c in np.cumsum([MOBA_W, MOBA_W, MOBA_W, SWA_W, SWA_KV_W])]
    for l in range(DEPTH):
        x = x + 0.5 * swiglu(rms_norm(x, g_ffn1[l]), w1_gate[l], w1_up[l], w1_down[l])
        n = rms_norm(x, g_mix[l])
        proj = n @ w_in[l] + b_in[l]
        qm, km, vm, qs, ksw, vsw = jnp.split(proj, splits, axis=-1)
        hm = (b_, s_, N_MOBA_HEADS, HEAD_DIM)
        o_m = moba_attention(qm.reshape(hm), km.reshape(hm), vm.reshape(hm), bias_moba)
        o_s = swa_sink_attention(qs.reshape(b_, s_, N_SWA_HEADS, HEAD_DIM),
                                 ksw.reshape(b_, s_, N_SWA_KV_HEADS, HEAD_DIM),
                                 vsw.reshape(b_, s_, N_SWA_KV_HEADS, HEAD_DIM),
                                 sinks[l], bias_swa)
        o = jnp.concatenate([rms_norm(o_m.reshape(b_, s_, MOBA_W), g_out_moba[l]),
                             rms_norm(o_s.reshape(b_, s_, SWA_W), g_out_swa[l])], axis=-1)
        x = x + o @ w_out[l]
        x = x + cross_attention(rms_norm(x, g_cross[l]), rms_norm(mem, g_mem[l]),
                                w_cq[l], w_ck[l], w_cv[l], w_co[l])
        x = x + 0.5 * swiglu(rms_norm(x, g_ffn2[l]), w2_gate[l], w2_up[l], w2_down[l])
    return rms_norm(x, g_final)
```

```python
import functools
import math

import numpy as np
import jax
import jax.numpy as jnp
from jax import lax
from jax.experimental import pallas as pl
from jax.experimental.pallas import tpu as pltpu

HEAD_DIM = 128
N_MOBA_HEADS = 16
N_SWA_HEADS = 16
N_SWA_KV_HEADS = 4
SWA_GROUP = N_SWA_HEADS // N_SWA_KV_HEADS
MOBA_BLOCK = 256
MOBA_TOPK = 3
SWA_WINDOW = 128
SWA_BLOCK = 128
N_CROSS_HEADS = 4
NUM_BUCKETS = 32
MAX_DISTANCE = 128
EPS = 1e-6

MOBA_W = N_MOBA_HEADS * HEAD_DIM
SWA_W = N_SWA_HEADS * HEAD_DIM
SWA_KV_W = N_SWA_KV_HEADS * HEAD_DIM
CROSS_W = N_CROSS_HEADS * HEAD_DIM

SLAB_QM = 0
SLAB_KM = SLAB_QM + N_MOBA_HEADS
SLAB_VM = SLAB_KM + N_MOBA_HEADS
SLAB_QS = SLAB_VM + N_MOBA_HEADS
SLAB_KS = SLAB_QS + N_SWA_HEADS
SLAB_VS = SLAB_KS + N_SWA_KV_HEADS
N_SLABS = SLAB_VS + N_SWA_KV_HEADS

MASKED = -1e30
LANES = 128
V7X_VMEM_LIMIT_BYTES = 56 * 1024 * 1024

F32 = jnp.float32
BF16 = jnp.bfloat16

_NT = (((1,), (1,)), ((), ()))
_TN = (((0,), (0,)), ((), ()))


def _tile(n, want):
    if n <= want:
        return n
    t = want
    while n % t:
        t -= 8
    return t


def _params(*semantics):
    return pltpu.CompilerParams(dimension_semantics=semantics, vmem_limit_bytes=V7X_VMEM_LIMIT_BYTES)


def _rms_rows(x, g):
    ms = jnp.mean(x * x, axis=-1, keepdims=True)
    return x * lax.rsqrt(ms + EPS) * g


def _rmsnorm_body(x_ref, g_ref, o_ref):
    o_ref[...] = _rms_rows(x_ref[...], g_ref[...]).astype(o_ref.dtype)


def _rmsnorm(x, g, out_dtype):
    m, d = x.shape
    tm = _tile(m, 256)
    return pl.pallas_call(
        _rmsnorm_body,
        grid=(m // tm,),
        in_specs=[pl.BlockSpec((tm, d), lambda i: (i, 0)), pl.BlockSpec((1, d), lambda i: (0, 0))],
        out_specs=pl.BlockSpec((tm, d), lambda i: (i, 0)),
        out_shape=jax.ShapeDtypeStruct((m, d), out_dtype),
        compiler_params=_params("parallel"),
        name="rmsnorm",
    )(x, g.reshape(1, d))


def _ffn_body(x_ref, g_ref, wg_ref, wu_ref, wd_ref, o_ref, h_ref, *, row_chunk):
    j = pl.program_id(1)
    tm = x_ref.shape[0]

    @pl.when(j == 0)
    def _():
        def norm_rows(r, carry):
            rows = pl.ds(pl.multiple_of(r * row_chunk, row_chunk), row_chunk)
            x = x_ref[rows, :]
            h_ref[rows, :] = _rms_rows(x, g_ref[...]).astype(BF16)
            o_ref[rows, :] = x
            return carry

        lax.fori_loop(0, tm // row_chunk, norm_rows, 0)

    h = h_ref[...]
    gate = jnp.dot(h, wg_ref[...], preferred_element_type=F32)
    up = jnp.dot(h, wu_ref[...], preferred_element_type=F32)
    act = (0.5 * gate * jax.nn.sigmoid(gate) * up).astype(BF16)
    o_ref[...] += jnp.dot(act, wd_ref[...], preferred_element_type=F32)


def _ffn(x, g, wg, wu, wd):
    m, d = x.shape
    f = wg.shape[1]
    tm = _tile(m, 512)
    tf = 256 if f % 256 == 0 else LANES
    body = functools.partial(_ffn_body, row_chunk=min(32, tm))
    return pl.pallas_call(
        body,
        grid=(m // tm, f // tf),
        in_specs=[
            pl.BlockSpec((tm, d), lambda i, j: (i, 0)),
            pl.BlockSpec((1, d), lambda i, j: (0, 0)),
            pl.BlockSpec((d, tf), lambda i, j: (0, j)),
            pl.BlockSpec((d, tf), lambda i, j: (0, j)),
            pl.BlockSpec((tf, d), lambda i, j: (j, 0)),
        ],
        out_specs=pl.BlockSpec((tm, d), lambda i, j: (i, 0)),
        out_shape=jax.ShapeDtypeStruct((m, d), F32),
        scratch_shapes=[pltpu.VMEM((tm, d), BF16)],
        compiler_params=_params("parallel", "arbitrary"),
        name="ffn",
    )(x, g.reshape(1, d), wg, wu, wd)


def _proj_slabs_body(a_ref, w_ref, b_ref, o_ref):
    acc = jnp.dot(a_ref[...], w_ref[...], preferred_element_type=F32) + b_ref[...]
    for c in range(o_ref.shape[0]):
        o_ref[c] = acc[:, c * LANES:(c + 1) * LANES].astype(o_ref.dtype)


def _proj_slabs(a, w, b):
    m, k = a.shape
    n = w.shape[1]
    tm = _tile(m, 1024)
    tn = _tile(n, 1024)
    return pl.pallas_call(
        _proj_slabs_body,
        grid=(m // tm, n // tn),
        in_specs=[
            pl.BlockSpec((tm, k), lambda i, j: (i, 0)),
            pl.BlockSpec((k, tn), lambda i, j: (0, j)),
            pl.BlockSpec((1, tn), lambda i, j: (0, j)),
        ],
        out_specs=pl.BlockSpec((tn // LANES, tm, LANES), lambda i, j: (j, i, 0)),
        out_shape=jax.ShapeDtypeStruct((n // LANES, m, LANES), BF16),
        compiler_params=_params("parallel", "parallel"),
        name="in_proj",
    )(a, w, b.reshape(1, n))


def _matmul_body(a_ref, w_ref, o_ref):
    o_ref[...] = jnp.dot(a_ref[...], w_ref[...], preferred_element_type=F32).astype(o_ref.dtype)


def _matmul(a, w, out_dtype):
    m, k = a.shape
    n = w.shape[1]
    tm = _tile(m, 1024)
    tn = _tile(n, 512)
    return pl.pallas_call(
        _matmul_body,
        grid=(m // tm, n // tn),
        in_specs=[pl.BlockSpec((tm, k), lambda i, j: (i, 0)), pl.BlockSpec((k, tn), lambda i, j: (0, j))],
        out_specs=pl.BlockSpec((tm, tn), lambda i, j: (i, j)),
        out_shape=jax.ShapeDtypeStruct((m, n), out_dtype),
        compiler_params=_params("parallel", "parallel"),
        name="matmul",
    )(a, w)


def _matmul_residual_body(a_ref, w_ref, r_ref, o_ref):
    o_ref[...] = r_ref[...] + jnp.dot(a_ref[...], w_ref[...], preferred_element_type=F32)


def _matmul_residual(a, w, res):
    m, k = a.shape
    n = w.shape[1]
    tm = _tile(m, 1024)
    tn = _tile(n, 512)
    return pl.pallas_call(
        _matmul_residual_body,
        grid=(m // tm, n // tn),
        in_specs=[
            pl.BlockSpec((tm, k), lambda i, j: (i, 0)),
            pl.BlockSpec((k, tn), lambda i, j: (0, j)),
            pl.BlockSpec((tm, tn), lambda i, j: (i, j)),
        ],
        out_specs=pl.BlockSpec((tm, tn), lambda i, j: (i, j)),
        out_shape=jax.ShapeDtypeStruct((m, n), F32),
        compiler_params=_params("parallel", "parallel"),
        name="out_proj",
    )(a, w, res)


def _online_update(s, v, m, l, acc):
    m_new = jnp.maximum(m, jnp.max(s, axis=0, keepdims=True))
    alpha = jnp.exp(m - m_new)
    p = jnp.exp(s - m_new)
    l = alpha * l + jnp.sum(p, axis=0, keepdims=True)
    acc = alpha * acc + lax.dot_general(v, p.astype(BF16), _TN, preferred_element_type=F32)
    return m_new, l, acc


def _moba_body(q_ref, k_ref, v_ref, town_ref, tprev_ref, bfar_ref, o_ref, km_ref, sel_ref, *, n_blocks, topk):
    n = pl.program_id(2)
    blk = MOBA_BLOCK
    scale = HEAD_DIM ** -0.5

    @pl.when(n == 0)
    def _():
        for j in range(n_blocks):
            kj = k_ref[0, 0, j * blk:(j + 1) * blk, :].astype(F32)
            km_ref[j:j + 1, :] = jnp.mean(kj, axis=0, keepdims=True)

    q = q_ref[0, 0]

    gate = lax.dot_general(km_ref[...].astype(BF16), q, _NT, preferred_element_type=F32)
    rows = lax.broadcasted_iota(jnp.int32, gate.shape, 0)
    gate = jnp.where(rows < n, gate, -jnp.inf)
    sel = jnp.zeros(gate.shape, jnp.bool_)
    for _ in range(topk):
        best = jnp.max(gate, axis=0, keepdims=True)
        first = jnp.min(jnp.where(gate == best, rows, n_blocks), axis=0, keepdims=True)
        pick = (rows == first) & (best > -jnp.inf)
        sel = sel | pick
        gate = jnp.where(pick, -jnp.inf, gate)
    sel_ref[...] = jnp.where(sel, 0.0, MASKED)

    def keys(j):
        rows_j = pl.ds(pl.multiple_of(j * blk, blk), blk)
        return k_ref[0, 0, rows_j, :], v_ref[0, 0, rows_j, :]

    def scores(kj):
        return lax.dot_general(kj, q, _NT, preferred_element_type=F32) * scale

    k_own, v_own = keys(n)
    s = scores(k_own) + town_ref[0]
    m = jnp.max(s, axis=0, keepdims=True)
    p = jnp.exp(s - m)
    l = jnp.sum(p, axis=0, keepdims=True)
    acc = lax.dot_general(v_own, p.astype(BF16), _TN, preferred_element_type=F32)

    jp = jnp.maximum(n - 1, 0)
    k_prev, v_prev = keys(jp)
    s = scores(k_prev) + tprev_ref[0] + sel_ref[pl.ds(jp, 1), :]
    m, l, acc = _online_update(s, v_prev, m, l, acc)

    def far_block(j, carry):
        kj, vj = keys(j)
        s = scores(kj) + (sel_ref[pl.ds(j, 1), :] + bfar_ref[0])
        return _online_update(s, vj, *carry)

    m, l, acc = lax.fori_loop(0, n - 1, far_block, (m, l, acc))
    o_ref[0, 0] = (acc / l).T.astype(o_ref.dtype)


def _moba(proj, town, tprev, bfar):
    _, b, s, _ = proj.shape
    blk = MOBA_BLOCK
    n_blocks = s // blk
    body = functools.partial(_moba_body, n_blocks=n_blocks, topk=min(MOBA_TOPK, n_blocks))
    return pl.pallas_call(
        body,
        grid=(b, N_MOBA_HEADS, n_blocks),
        in_specs=[
            pl.BlockSpec((1, 1, blk, HEAD_DIM), lambda bi, h, n: (SLAB_QM + h, bi, n, 0)),
            pl.BlockSpec((1, 1, s, HEAD_DIM), lambda bi, h, n: (SLAB_KM + h, bi, 0, 0)),
            pl.BlockSpec((1, 1, s, HEAD_DIM), lambda bi, h, n: (SLAB_VM + h, bi, 0, 0)),
            pl.BlockSpec((1, blk, blk), lambda bi, h, n: (h, 0, 0)),
            pl.BlockSpec((1, blk, blk), lambda bi, h, n: (h, 0, 0)),
            pl.BlockSpec((1, 1, blk), lambda bi, h, n: (h, 0, 0)),
        ],
        out_specs=pl.BlockSpec((1, 1, blk, HEAD_DIM), lambda bi, h, n: (h, bi, n, 0)),
        out_shape=jax.ShapeDtypeStruct((N_MOBA_HEADS, b, s, HEAD_DIM), F32),
        scratch_shapes=[pltpu.VMEM((n_blocks, HEAD_DIM), F32), pltpu.VMEM((n_blocks, blk), F32)],
        compiler_params=_params("parallel", "parallel", "arbitrary"),
        name="moba",
    )(proj, proj, proj, town, tprev, bfar)


def _swa_body(q_ref, k_ref, v_ref, bias_ref, sink_ref, o_ref):
    n = pl.program_id(2)
    blk = SWA_BLOCK
    scale = HEAD_DIM ** -0.5
    q = jnp.concatenate([q_ref[g, 0] for g in range(SWA_GROUP)], axis=0)
    prev_rows = pl.ds(pl.multiple_of(jnp.maximum(n - 1, 0) * blk, blk), blk)
    own_rows = pl.ds(pl.multiple_of(n * blk, blk), blk)
    k = jnp.concatenate([k_ref[0, 0, prev_rows, :], k_ref[0, 0, own_rows, :]], axis=0)
    v = jnp.concatenate([v_ref[0, 0, prev_rows, :], v_ref[0, 0, own_rows, :]], axis=0)
    s = lax.dot_general(k, q, _NT, preferred_element_type=F32) * scale + bias_ref[0]
    key_row = lax.broadcasted_iota(jnp.int32, s.shape, 0)
    s = jnp.where((key_row < blk) & (n == 0), MASKED, s)
    sink = sink_ref[0]
    m = jnp.maximum(jnp.max(s, axis=0, keepdims=True), sink)
    p = jnp.exp(s - m)
    l = jnp.sum(p, axis=0, keepdims=True) + jnp.exp(sink - m)
    o = lax.dot_general(v, p.astype(BF16), _TN, preferred_element_type=F32) / l
    for g in range(SWA_GROUP):
        o_ref[g, 0] = o[:, g * blk:(g + 1) * blk].T.astype(o_ref.dtype)


def _swa(proj, bias, sink):
    _, b, s, _ = proj.shape
    blk = SWA_BLOCK
    gq = SWA_GROUP * blk
    return pl.pallas_call(
        _swa_body,
        grid=(b, N_SWA_KV_HEADS, s // blk),
        in_specs=[
            pl.BlockSpec((SWA_GROUP, 1, blk, HEAD_DIM), lambda bi, h, n: (SLAB_QS // SWA_GROUP + h, bi, n, 0)),
            pl.BlockSpec((1, 1, s, HEAD_DIM), lambda bi, h, n: (SLAB_KS + h, bi, 0, 0)),
            pl.BlockSpec((1, 1, s, HEAD_DIM), lambda bi, h, n: (SLAB_VS + h, bi, 0, 0)),
            pl.BlockSpec((1, 2 * blk, gq), lambda bi, h, n: (h, 0, 0)),
            pl.BlockSpec((1, 1, gq), lambda bi, h, n: (h, 0, 0)),
        ],
        out_specs=pl.BlockSpec((SWA_GROUP, 1, blk, HEAD_DIM), lambda bi, h, n: (h, bi, n, 0)),
        out_shape=jax.ShapeDtypeStruct((N_SWA_HEADS, b, s, HEAD_DIM), F32),
        compiler_params=_params("parallel", "parallel", "arbitrary"),
        name="swa",
    )(proj, proj, proj, bias, sink)


def _head_norm_body(om_ref, os_ref, gm_ref, gs_ref, o_ref):
    col = 0
    for x_ref, g_ref in ((om_ref, gm_ref), (os_ref, gs_ref)):
        heads = x_ref.shape[0]
        ss = sum(jnp.sum(jnp.square(x_ref[c]), axis=-1, keepdims=True) for c in range(heads))
        r = lax.rsqrt(ss / (heads * HEAD_DIM) + EPS)
        for c in range(heads):
            o_ref[:, col:col + HEAD_DIM] = (x_ref[c] * r * g_ref[:, c * HEAD_DIM:(c + 1) * HEAD_DIM]).astype(o_ref.dtype)
            col += HEAD_DIM


def _head_norm(om, osw, gm, gs):
    hm, m, _ = om.shape
    hs = osw.shape[0]
    tm = _tile(m, 256)
    width = (hm + hs) * HEAD_DIM
    return pl.pallas_call(
        _head_norm_body,
        grid=(m // tm,),
        in_specs=[
            pl.BlockSpec((hm, tm, HEAD_DIM), lambda i: (0, i, 0)),
            pl.BlockSpec((hs, tm, HEAD_DIM), lambda i: (0, i, 0)),
            pl.BlockSpec((1, hm * HEAD_DIM), lambda i: (0, 0)),
            pl.BlockSpec((1, hs * HEAD_DIM), lambda i: (0, 0)),
        ],
        out_specs=pl.BlockSpec((tm, width), lambda i: (i, 0)),
        out_shape=jax.ShapeDtypeStruct((m, width), BF16),
        compiler_params=_params("parallel"),
        name="head_norm",
    )(om, osw, gm.reshape(1, -1), gs.reshape(1, -1))


def _cross_body(h_ref, wq_ref, kv_ref, wo_ref, x_ref, o_ref):
    scale = HEAD_DIM ** -0.5
    q = jnp.dot(h_ref[...], wq_ref[...], preferred_element_type=F32).astype(BF16)
    outs = []
    for hd in range(N_CROSS_HEADS):
        cols = slice(hd * HEAD_DIM, (hd + 1) * HEAD_DIM)
        k = kv_ref[0, :, cols]
        v = kv_ref[0, :, CROSS_W + hd * HEAD_DIM:CROSS_W + (hd + 1) * HEAD_DIM]
        s = lax.dot_general(q[:, cols], k, _NT, preferred_element_type=F32) * scale
        p = jnp.exp(s - jnp.max(s, axis=-1, keepdims=True))
        l = jnp.sum(p, axis=-1, keepdims=True)
        outs.append((jnp.dot(p.astype(BF16), v, preferred_element_type=F32) / l).astype(BF16))
    o = jnp.concatenate(outs, axis=-1)
    o_ref[...] = x_ref[...] + jnp.dot(o, wo_ref[...], preferred_element_type=F32)


def _cross(h, wq, kv, wo, x, seq):
    m, d = h.shape
    n_mem = kv.shape[1]
    tm = _tile(seq, 256)
    tiles_per_seq = seq // tm
    return pl.pallas_call(
        _cross_body,
        grid=(m // tm,),
        in_specs=[
            pl.BlockSpec((tm, d), lambda i: (i, 0)),
            pl.BlockSpec((d, CROSS_W), lambda i: (0, 0)),
            pl.BlockSpec((1, n_mem, 2 * CROSS_W), lambda i: (i // tiles_per_seq, 0, 0)),
            pl.BlockSpec((CROSS_W, d), lambda i: (0, 0)),
            pl.BlockSpec((tm, d), lambda i: (i, 0)),
        ],
        out_specs=pl.BlockSpec((tm, d), lambda i: (i, 0)),
        out_shape=jax.ShapeDtypeStruct((m, d), F32),
        compiler_params=_params("parallel"),
        name="cross",
    )(h, wq, kv, wo, x)


def _t5_bucket(dist):
    n = np.maximum(dist, 0)
    max_exact = NUM_BUCKETS // 2
    nf = np.maximum(n, max_exact).astype(np.float32)
    large = max_exact + (np.log(nf / np.float32(max_exact)) / np.float32(math.log(MAX_DISTANCE / max_exact))
                         * np.float32(NUM_BUCKETS - max_exact)).astype(np.int32)
    large = np.minimum(large, NUM_BUCKETS - 1)
    return np.where(n < max_exact, n, large).astype(np.int32)


def _moba_bias_tables(bias_hb):
    blk = MOBA_BLOCK
    key = np.arange(blk)[:, None]
    qry = np.arange(blk)[None, :]
    d_own = qry - key
    town = jnp.where(d_own >= 0, bias_hb[:, _t5_bucket(d_own)], MASKED)
    tprev = bias_hb[:, _t5_bucket(d_own + blk)]
    far = int(_t5_bucket(np.array(blk + 1)))
    bfar = jnp.broadcast_to(bias_hb[:, far][:, None, None], (bias_hb.shape[0], 1, blk))
    return town, tprev, bfar


def _swa_bias_table(bias_hb):
    blk = SWA_BLOCK
    key = np.arange(2 * blk)[:, None]
    qry = np.arange(blk)[None, :]
    dist = qry + blk - key
    ok = (dist >= 0) & (dist < SWA_WINDOW)
    t = jnp.where(ok, bias_hb[:, _t5_bucket(dist)], MASKED)
    t = t.reshape(N_SWA_KV_HEADS, SWA_GROUP, 2 * blk, blk).transpose(0, 2, 1, 3)
    return t.reshape(N_SWA_KV_HEADS, 2 * blk, SWA_GROUP * blk)


def kernel(x, mem, rel_bias, g_final, g_ffn1, w1_gate, w1_up, w1_down, g_mix, w_in, b_in, sinks, g_out_moba, g_out_swa, w_out, g_cross, g_mem, w_cq, w_ck, w_cv, w_co, g_ffn2, w2_gate, w2_up, w2_down):
    b, s, d = x.shape
    n_mem = mem.shape[1]
    depth = w_in.shape[0]
    assert s % MOBA_BLOCK == 0 and s % SWA_BLOCK == 0 and d % LANES == 0

    bias_moba = rel_bias[:, :N_MOBA_HEADS].T.astype(F32)
    bias_swa = rel_bias[:, N_MOBA_HEADS:].T.astype(F32)
    town, tprev, bfar = _moba_bias_tables(bias_moba)
    tswa = _swa_bias_table(bias_swa)

    xt = x.reshape(b * s, d)
    memt = mem.reshape(b * n_mem, d)
    for l in range(depth):
        xt = _ffn(xt, g_ffn1[l], w1_gate[l].astype(BF16), w1_up[l].astype(BF16), w1_down[l].astype(BF16))

        nrm = _rmsnorm(xt, g_mix[l], BF16)
        proj = _proj_slabs(nrm, w_in[l].astype(BF16), b_in[l]).reshape(N_SLABS, b, s, HEAD_DIM)
        o_m = _moba(proj, town, tprev, bfar).reshape(N_MOBA_HEADS, b * s, HEAD_DIM)
        sink = jnp.broadcast_to(sinks[l].astype(F32).reshape(N_SWA_KV_HEADS, SWA_GROUP, 1),
                                (N_SWA_KV_HEADS, SWA_GROUP, SWA_BLOCK)).reshape(N_SWA_KV_HEADS, 1, SWA_GROUP * SWA_BLOCK)
        o_s = _swa(proj, tswa, sink).reshape(N_SWA_HEADS, b * s, HEAD_DIM)
        o = _head_norm(o_m, o_s, g_out_moba[l], g_out_swa[l])
        xt = _matmul_residual(o, w_out[l].astype(BF16), xt)

        hc = _rmsnorm(xt, g_cross[l], BF16)
        mem_n = _rmsnorm(memt, g_mem[l], BF16)
        w_kv = jnp.concatenate([w_ck[l], w_cv[l]], axis=1).astype(BF16)
        kv = _matmul(mem_n, w_kv, BF16).reshape(b, n_mem, 2 * CROSS_W)
        xt = _cross(hc, w_cq[l].astype(BF16), kv, w_co[l].astype(BF16), xt, s)

        xt = _ffn(xt, g_ffn2[l], w2_gate[l].astype(BF16), w2_up[l].astype(BF16), w2_down[l].astype(BF16))
    return _rmsnorm(xt, g_final, F32).reshape(b, s, d)
```

```python
import functools
import math

import numpy as np
import jax
import jax.numpy as jnp
from jax import lax
from jax.experimental import pallas as pl
from jax.experimental.pallas import tpu as pltpu

HEAD_DIM = 128
N_MOBA_HEADS = 16
N_SWA_HEADS = 16
N_SWA_KV_HEADS = 4
SWA_GROUP = N_SWA_HEADS // N_SWA_KV_HEADS
MOBA_BLOCK = 256
MOBA_TOPK = 3
SWA_WINDOW = 128
SWA_BLOCK = 128
N_CROSS_HEADS = 4
NUM_BUCKETS = 32
MAX_DISTANCE = 128
EPS = 1e-6

MOBA_W = N_MOBA_HEADS * HEAD_DIM
SWA_W = N_SWA_HEADS * HEAD_DIM
SWA_KV_W = N_SWA_KV_HEADS * HEAD_DIM
CROSS_W = N_CROSS_HEADS * HEAD_DIM

SLAB_QM = 0
SLAB_KM = SLAB_QM + N_MOBA_HEADS
SLAB_VM = SLAB_KM + N_MOBA_HEADS
SLAB_QS = SLAB_VM + N_MOBA_HEADS
SLAB_KS = SLAB_QS + N_SWA_HEADS
SLAB_VS = SLAB_KS + N_SWA_KV_HEADS
N_SLABS = SLAB_VS + N_SWA_KV_HEADS

MASKED = -1e30
LANES = 128
V7X_VMEM_LIMIT_BYTES = 56 * 1024 * 1024

F32 = jnp.float32
BF16 = jnp.bfloat16

_NT = (((1,), (1,)), ((), ()))
_TN = (((0,), (0,)), ((), ()))


def _tile(n, want):
    if n <= want:
        return n
    t = want
    while n % t:
        t -= 8
    return t


def _params(*semantics):
    return pltpu.CompilerParams(dimension_semantics=semantics, vmem_limit_bytes=V7X_VMEM_LIMIT_BYTES)


def _rms_rows(x, g):
    ms = jnp.mean(x * x, axis=-1, keepdims=True)
    return x * lax.rsqrt(ms + EPS) * g


def _rmsnorm_body(x_ref, g_ref, o_ref):
    o_ref[...] = _rms_rows(x_ref[...], g_ref[...]).astype(o_ref.dtype)


def _rmsnorm(x, g, out_dtype):
    m, d = x.shape
    tm = _tile(m, 256)
    return pl.pallas_call(
        _rmsnorm_body,
        grid=(m // tm,),
        in_specs=[pl.BlockSpec((tm, d), lambda i: (i, 0)), pl.BlockSpec((1, d), lambda i: (0, 0))],
        out_specs=pl.BlockSpec((tm, d), lambda i: (i, 0)),
        out_shape=jax.ShapeDtypeStruct((m, d), out_dtype),
        compiler_params=_params("parallel"),
        name="rmsnorm",
    )(x, g.reshape(1, d))


def _ffn_body(x_ref, g_ref, wg_ref, wu_ref, wd_ref, o_ref, h_ref, *, row_chunk):
    j = pl.program_id(1)
    tm = x_ref.shape[0]

    @pl.when(j == 0)
    def _():
        def norm_rows(r, carry):
            rows = pl.ds(pl.multiple_of(r * row_chunk, row_chunk), row_chunk)
            x = x_ref[rows, :]
            h_ref[rows, :] = _rms_rows(x, g_ref[...]).astype(BF16)
            o_ref[rows, :] = x
            return carry

        lax.fori_loop(0, tm // row_chunk, norm_rows, 0)

    h = h_ref[...]
    gate = jnp.dot(h, wg_ref[...], preferred_element_type=F32)
    up = jnp.dot(h, wu_ref[...], preferred_element_type=F32)
    act = (0.5 * gate * jax.nn.sigmoid(gate) * up).astype(BF16)
    o_ref[...] += jnp.dot(act, wd_ref[...], preferred_element_type=F32)


def _ffn(x, g, wg, wu, wd):
    m, d = x.shape
    f = wg.shape[1]
    tm = _tile(m, 512)
    tf = 256 if f % 256 == 0 else LANES
    body = functools.partial(_ffn_body, row_chunk=min(32, tm))
    return pl.pallas_call(
        body,
        grid=(m // tm, f // tf),
        in_specs=[
            pl.BlockSpec((tm, d), lambda i, j: (i, 0)),
            pl.BlockSpec((1, d), lambda i, j: (0, 0)),
            pl.BlockSpec((d, tf), lambda i, j: (0, j)),
            pl.BlockSpec((d, tf), lambda i, j: (0, j)),
            pl.BlockSpec((tf, d), lambda i, j: (j, 0)),
        ],
        out_specs=pl.BlockSpec((tm, d), lambda i, j: (i, 0)),
        out_shape=jax.ShapeDtypeStruct((m, d), F32),
        scratch_shapes=[pltpu.VMEM((tm, d), BF16)],
        compiler_params=_params("parallel", "arbitrary"),
        name="ffn",
    )(x, g.reshape(1, d), wg, wu, wd)


def _proj_slabs_body(a_ref, w_ref, b_ref, o_ref):
    acc = jnp.dot(a_ref[...], w_ref[...], preferred_element_type=F32) + b_ref[...]
    for c in range(o_ref.shape[0]):
        o_ref[c] = acc[:, c * LANES:(c + 1) * LANES].astype(o_ref.dtype)


def _proj_slabs(a, w, b):
    m, k = a.shape
    n = w.shape[1]
    tm = _tile(m, 1024)
    tn = _tile(n, 1024)
    return pl.pallas_call(
        _proj_slabs_body,
        grid=(m // tm, n // tn),
        in_specs=[
            pl.BlockSpec((tm, k), lambda i, j: (i, 0)),
            pl.BlockSpec((k, tn), lambda i, j: (0, j)),
            pl.BlockSpec((1, tn), lambda i, j: (0, j)),
        ],
        out_specs=pl.BlockSpec((tn // LANES, tm, LANES), lambda i, j: (j, i, 0)),
        out_shape=jax.ShapeDtypeStruct((n // LANES, m, LANES), BF16),
        compiler_params=_params("parallel", "parallel"),
        name="in_proj",
    )(a, w, b.reshape(1, n))


def _matmul_body(a_ref, w_ref, o_ref):
    o_ref[...] = jnp.dot(a_ref[...], w_ref[...], preferred_element_type=F32).astype(o_ref.dtype)


def _matmul(a, w, out_dtype):
    m, k = a.shape
    n = w.shape[1]
    tm = _tile(m, 1024)
    tn = _tile(n, 512)
    return pl.pallas_call(
        _matmul_body,
        grid=(m // tm, n // tn),
        in_specs=[pl.BlockSpec((tm, k), lambda i, j: (i, 0)), pl.BlockSpec((k, tn), lambda i, j: (0, j))],
        out_specs=pl.BlockSpec((tm, tn), lambda i, j: (i, j)),
        out_shape=jax.ShapeDtypeStruct((m, n), out_dtype),
        compiler_params=_params("parallel", "parallel"),
        name="matmul",
    )(a, w)


def _matmul_residual_body(a_ref, w_ref, r_ref, o_ref):
    o_ref[...] = r_ref[...] + jnp.dot(a_ref[...], w_ref[...], preferred_element_type=F32)


def _matmul_residual(a, w, res):
    m, k = a.shape
    n = w.shape[1]
    tm = _tile(m, 1024)
    tn = _tile(n, 512)
    return pl.pallas_call(
        _matmul_residual_body,
        grid=(m // tm, n // tn),
        in_specs=[
            pl.BlockSpec((tm, k), lambda i, j: (i, 0)),
            pl.BlockSpec((k, tn), lambda i, j: (0, j)),
            pl.BlockSpec((tm, tn), lambda i, j: (i, j)),
        ],
        out_specs=pl.BlockSpec((tm, tn), lambda i, j: (i, j)),
        out_shape=jax.ShapeDtypeStruct((m, n), F32),
        compiler_params=_params("parallel", "parallel"),
        name="out_proj",
    )(a, w, res)


def _fold_rows(x, op, rows=8):
    while x.shape[0] > rows:
        half = x.shape[0] // 2
        x = op(x[:half], x[half:])
    return x


def _moba_body(q_ref, k_ref, v_ref, town_ref, tprev_ref, bfar_ref, o_ref, km_ref, sel_ref, s_ref, l_ref, acc_ref,
               *, n_blocks, topk, heads):
    n = pl.program_id(2)
    blk = MOBA_BLOCK
    pair = 2 * blk
    scale = HEAD_DIM ** -0.5
    group = range(heads)

    @pl.when(n == 0)
    def _():
        for g in group:
            for j in range(n_blocks):
                kj = k_ref[g, 0, j * blk:(j + 1) * blk, :].astype(F32)
                km_ref[g, j:j + 1, :] = jnp.mean(kj, axis=0, keepdims=True)

    q = [q_ref[g, 0] for g in group]

    for g in group:
        gate = lax.dot_general(km_ref[g].astype(BF16), q[g], _NT, preferred_element_type=F32)
        rows = lax.broadcasted_iota(jnp.int32, gate.shape, 0)
        gate = jnp.where(rows < n, gate, -jnp.inf)
        sel = jnp.zeros(gate.shape, jnp.bool_)
        for _ in range(topk):
            best = jnp.max(gate, axis=0, keepdims=True)
            first = jnp.min(jnp.where(gate == best, rows, n_blocks), axis=0, keepdims=True)
            pick = (rows == first) & (best > -jnp.inf)
            sel = sel | pick
            gate = jnp.where(pick, -jnp.inf, gate)
        sel_ref[g] = jnp.where(sel, 0.0, MASKED)

    def scores(g, rows):
        return lax.dot_general(k_ref[g, 0, rows, :], q[g], _NT, preferred_element_type=F32) * scale

    def pv(g, rows, p):
        return lax.dot_general(v_ref[g, 0, rows, :], p.astype(BF16), _TN, preferred_element_type=F32)

    jp = jnp.maximum(n - 1, 0)
    own_rows = pl.ds(pl.multiple_of(n * blk, blk), blk)
    prev_rows = pl.ds(pl.multiple_of(jp * blk, blk), blk)
    m8 = []
    for g in group:
        s_own = scores(g, own_rows) + town_ref[g]
        s_prev = scores(g, prev_rows) + tprev_ref[g] + sel_ref[g, pl.ds(jp, 1), :]
        s_ref[g, 0:blk, :] = s_own
        s_ref[g, blk:pair, :] = s_prev
        m8.append(jnp.maximum(_fold_rows(s_own, jnp.maximum), _fold_rows(s_prev, jnp.maximum)))

    n_far = n - 1
    n_pairs = lax.shift_right_logical(n, 1)

    def far_scores(c, m8):
        key_rows = pl.ds(pl.multiple_of(c * pair, pair), pair)
        base = pl.multiple_of(pair + c * pair, pair)
        j1 = 2 * c + 1
        out = []
        for g in group:
            s = scores(g, key_rows)
            bfar = bfar_ref[g]
            s0 = s[:blk] + (sel_ref[g, pl.ds(2 * c, 1), :] + bfar)
            s1 = s[blk:] + jnp.where(j1 < n_far, sel_ref[g, pl.ds(j1, 1), :] + bfar, MASKED)
            s_ref[g, pl.ds(base, blk), :] = s0
            s_ref[g, pl.ds(base + blk, blk), :] = s1
            out.append(jnp.maximum(m8[g], jnp.maximum(_fold_rows(s0, jnp.maximum), _fold_rows(s1, jnp.maximum))))
        return tuple(out)

    m8 = lax.fori_loop(0, n_pairs, far_scores, tuple(m8))
    m = [jnp.max(m8[g], axis=0, keepdims=True) for g in group]

    for g in group:
        p_own = jnp.exp(s_ref[g, 0:blk, :] - m[g])
        p_prev = jnp.exp(s_ref[g, blk:pair, :] - m[g])
        l_ref[g] = _fold_rows(p_own, jnp.add) + _fold_rows(p_prev, jnp.add)
        acc_ref[g] = pv(g, own_rows, p_own) + pv(g, prev_rows, p_prev)

    def far_pv(c, carry):
        key_rows = pl.ds(pl.multiple_of(c * pair, pair), pair)
        base = pl.multiple_of(pair + c * pair, pair)
        for g in group:
            p = jnp.exp(s_ref[g, pl.ds(base, pair), :] - m[g])
            l_ref[g] += _fold_rows(p, jnp.add)
            acc_ref[g] += pv(g, key_rows, p)
        return carry

    lax.fori_loop(0, n_pairs, far_pv, 0)
    for g in group:
        l = jnp.sum(l_ref[g], axis=0, keepdims=True)
        o_ref[g, 0] = (acc_ref[g] / l).T.astype(o_ref.dtype)


def _moba(proj, town, tprev, bfar):
    _, b, s, _ = proj.shape
    blk = MOBA_BLOCK
    n_blocks = s // blk
    heads = 4
    body = functools.partial(_moba_body, n_blocks=n_blocks, topk=min(MOBA_TOPK, n_blocks), heads=heads)
    return pl.pallas_call(
        body,
        grid=(b, N_MOBA_HEADS // heads, n_blocks),
        in_specs=[
            pl.BlockSpec((heads, 1, blk, HEAD_DIM), lambda bi, h, n: (SLAB_QM // heads + h, bi, n, 0)),
            pl.BlockSpec((heads, 1, s, HEAD_DIM), lambda bi, h, n: (SLAB_KM // heads + h, bi, 0, 0)),
            pl.BlockSpec((heads, 1, s, HEAD_DIM), lambda bi, h, n: (SLAB_VM // heads + h, bi, 0, 0)),
            pl.BlockSpec((heads, blk, blk), lambda bi, h, n: (h, 0, 0)),
            pl.BlockSpec((heads, blk, blk), lambda bi, h, n: (h, 0, 0)),
            pl.BlockSpec((heads, 1, blk), lambda bi, h, n: (h, 0, 0)),
        ],
        out_specs=pl.BlockSpec((heads, 1, blk, HEAD_DIM), lambda bi, h, n: (h, bi, n, 0)),
        out_shape=jax.ShapeDtypeStruct((N_MOBA_HEADS, b, s, HEAD_DIM), F32),
        scratch_shapes=[
            pltpu.VMEM((heads, n_blocks, HEAD_DIM), F32),
            pltpu.VMEM((heads, n_blocks, blk), F32),
            pltpu.VMEM((heads, 2 * blk * (1 + (n_blocks - 1) // 2), blk), F32),
            pltpu.VMEM((heads, 8, blk), F32),
            pltpu.VMEM((heads, HEAD_DIM, blk), F32),
        ],
        compiler_params=_params("parallel", "parallel", "arbitrary"),
        name="moba",
    )(proj, proj, proj, town, tprev, bfar)


def _swa_body(q_ref, k_ref, v_ref, bias_ref, sink_ref, o_ref, *, blocks_per_step):
    blk = SWA_BLOCK
    scale = HEAD_DIM ** -0.5
    sink = sink_ref[0]
    for r in range(blocks_per_step):
        n = pl.program_id(2) * blocks_per_step + r
        q_rows = slice(r * blk, (r + 1) * blk)
        q = jnp.concatenate([q_ref[g, 0, q_rows, :] for g in range(SWA_GROUP)], axis=0)
        prev_rows = pl.ds(pl.multiple_of(jnp.maximum(n - 1, 0) * blk, blk), blk)
        own_rows = pl.ds(pl.multiple_of(n * blk, blk), blk)
        k = jnp.concatenate([k_ref[0, 0, prev_rows, :], k_ref[0, 0, own_rows, :]], axis=0)
        v = jnp.concatenate([v_ref[0, 0, prev_rows, :], v_ref[0, 0, own_rows, :]], axis=0)
        s = lax.dot_general(k, q, _NT, preferred_element_type=F32) * scale + bias_ref[0]
        key_row = lax.broadcasted_iota(jnp.int32, s.shape, 0)
        s = jnp.where((key_row < blk) & (n == 0), MASKED, s)
        m = jnp.maximum(jnp.max(s, axis=0, keepdims=True), sink)
        p = jnp.exp(s - m)
        l = jnp.sum(p, axis=0, keepdims=True) + jnp.exp(sink - m)
        o = lax.dot_general(v, p.astype(BF16), _TN, preferred_element_type=F32) / l
        for g in range(SWA_GROUP):
            o_ref[g, 0, q_rows, :] = o[:, g * blk:(g + 1) * blk].T.astype(o_ref.dtype)


def _swa(proj, bias, sink):
    _, b, s, _ = proj.shape
    blk = SWA_BLOCK
    gq = SWA_GROUP * blk
    per_step = 4 if (s // blk) % 4 == 0 else 1
    rows = per_step * blk
    return pl.pallas_call(
        functools.partial(_swa_body, blocks_per_step=per_step),
        grid=(b, N_SWA_KV_HEADS, s // rows),
        in_specs=[
            pl.BlockSpec((SWA_GROUP, 1, rows, HEAD_DIM), lambda bi, h, n: (SLAB_QS // SWA_GROUP + h, bi, n, 0)),
            pl.BlockSpec((1, 1, s, HEAD_DIM), lambda bi, h, n: (SLAB_KS + h, bi, 0, 0)),
            pl.BlockSpec((1, 1, s, HEAD_DIM), lambda bi, h, n: (SLAB_VS + h, bi, 0, 0)),
            pl.BlockSpec((1, 2 * blk, gq), lambda bi, h, n: (h, 0, 0)),
            pl.BlockSpec((1, 1, gq), lambda bi, h, n: (h, 0, 0)),
        ],
        out_specs=pl.BlockSpec((SWA_GROUP, 1, rows, HEAD_DIM), lambda bi, h, n: (h, bi, n, 0)),
        out_shape=jax.ShapeDtypeStruct((N_SWA_HEADS, b, s, HEAD_DIM), F32),
        compiler_params=_params("parallel", "parallel", "arbitrary"),
        name="swa",
    )(proj, proj, proj, bias, sink)


def _head_norm_body(om_ref, os_ref, gm_ref, gs_ref, o_ref):
    col = 0
    for x_ref, g_ref in ((om_ref, gm_ref), (os_ref, gs_ref)):
        heads = x_ref.shape[0]
        ss = sum(jnp.sum(jnp.square(x_ref[c]), axis=-1, keepdims=True) for c in range(heads))
        r = lax.rsqrt(ss / (heads * HEAD_DIM) + EPS)
        for c in range(heads):
            o_ref[:, col:col + HEAD_DIM] = (x_ref[c] * r * g_ref[:, c * HEAD_DIM:(c + 1) * HEAD_DIM]).astype(o_ref.dtype)
            col += HEAD_DIM


def _head_norm(om, osw, gm, gs):
    hm, m, _ = om.shape
    hs = osw.shape[0]
    tm = _tile(m, 256)
    width = (hm + hs) * HEAD_DIM
    return pl.pallas_call(
        _head_norm_body,
        grid=(m // tm,),
        in_specs=[
            pl.BlockSpec((hm, tm, HEAD_DIM), lambda i: (0, i, 0)),
            pl.BlockSpec((hs, tm, HEAD_DIM), lambda i: (0, i, 0)),
            pl.BlockSpec((1, hm * HEAD_DIM), lambda i: (0, 0)),
            pl.BlockSpec((1, hs * HEAD_DIM), lambda i: (0, 0)),
        ],
        out_specs=pl.BlockSpec((tm, width), lambda i: (i, 0)),
        out_shape=jax.ShapeDtypeStruct((m, width), BF16),
        compiler_params=_params("parallel"),
        name="head_norm",
    )(om, osw, gm.reshape(1, -1), gs.reshape(1, -1))


def _cross_body(h_ref, wq_ref, kv_ref, wo_ref, x_ref, o_ref):
    scale = HEAD_DIM ** -0.5
    q = jnp.dot(h_ref[...], wq_ref[...], preferred_element_type=F32).astype(BF16)
    outs = []
    for hd in range(N_CROSS_HEADS):
        cols = slice(hd * HEAD_DIM, (hd + 1) * HEAD_DIM)
        k = kv_ref[0, :, cols]
        v = kv_ref[0, :, CROSS_W + hd * HEAD_DIM:CROSS_W + (hd + 1) * HEAD_DIM]
        s = lax.dot_general(q[:, cols], k, _NT, preferred_element_type=F32) * scale
        p = jnp.exp(s - jnp.max(s, axis=-1, keepdims=True))
        l = jnp.sum(p, axis=-1, keepdims=True)
        outs.append((jnp.dot(p.astype(BF16), v, preferred_element_type=F32) / l).astype(BF16))
    o = jnp.concatenate(outs, axis=-1)
    o_ref[...] = x_ref[...] + jnp.dot(o, wo_ref[...], preferred_element_type=F32)


def _cross(h, wq, kv, wo, x, seq):
    m, d = h.shape
    n_mem = kv.shape[1]
    tm = _tile(seq, 256)
    tiles_per_seq = seq // tm
    return pl.pallas_call(
        _cross_body,
        grid=(m // tm,),
        in_specs=[
            pl.BlockSpec((tm, d), lambda i: (i, 0)),
            pl.BlockSpec((d, CROSS_W), lambda i: (0, 0)),
            pl.BlockSpec((1, n_mem, 2 * CROSS_W), lambda i: (i // tiles_per_seq, 0, 0)),
            pl.BlockSpec((CROSS_W, d), lambda i: (0, 0)),
            pl.BlockSpec((tm, d), lambda i: (i, 0)),
        ],
        out_specs=pl.BlockSpec((tm, d), lambda i: (i, 0)),
        out_shape=jax.ShapeDtypeStruct((m, d), F32),
        compiler_params=_params("parallel"),
        name="cross",
    )(h, wq, kv, wo, x)


def _t5_bucket(dist):
    n = np.maximum(dist, 0)
    max_exact = NUM_BUCKETS // 2
    nf = np.maximum(n, max_exact).astype(np.float32)
    large = max_exact + (np.log(nf / np.float32(max_exact)) / np.float32(math.log(MAX_DISTANCE / max_exact))
                         * np.float32(NUM_BUCKETS - max_exact)).astype(np.int32)
    large = np.minimum(large, NUM_BUCKETS - 1)
    return np.where(n < max_exact, n, large).astype(np.int32)


def _bias_by_distance(bias_hb, dists):
    onehot = _t5_bucket(dists)[None, :] == np.arange(NUM_BUCKETS)[:, None]
    return jnp.sum(jnp.where(onehot[None], bias_hb[:, :, None], 0.0), axis=1)


def _toeplitz(g, rows, cols):
    heads, n = g.shape
    assert n == rows + cols - 1
    width = rows + cols
    tiled = jnp.broadcast_to(jnp.pad(g, ((0, 0), (0, 1)))[:, None, :], (heads, rows, width))
    skew = tiled.reshape(heads, rows * width)[:, :rows * (width - 1)].reshape(heads, rows, width - 1)
    return skew[:, :, rows - 1:rows - 1 + cols]


def _moba_bias_tables(bias_hb):
    blk = MOBA_BLOCK
    d_own = np.arange(2 * blk - 1) - (blk - 1)
    g_own = jnp.where(d_own >= 0, _bias_by_distance(bias_hb, np.maximum(d_own, 0)), MASKED)
    g_prev = _bias_by_distance(bias_hb, d_own + blk)
    far = int(_t5_bucket(np.array(blk + 1)))
    bfar = jnp.broadcast_to(bias_hb[:, far][:, None, None], (bias_hb.shape[0], 1, blk))
    return _toeplitz(g_own, blk, blk), _toeplitz(g_prev, blk, blk), bfar


def _swa_bias_table(bias_hb):
    blk = SWA_BLOCK
    dist = np.arange(3 * blk - 1) - (blk - 1)
    ok = (dist >= 0) & (dist < SWA_WINDOW)
    g = jnp.where(ok, _bias_by_distance(bias_hb, np.maximum(dist, 0)), MASKED)
    t = _toeplitz(g, 2 * blk, blk)
    t = t.reshape(N_SWA_KV_HEADS, SWA_GROUP, 2 * blk, blk).transpose(0, 2, 1, 3)
    return t.reshape(N_SWA_KV_HEADS, 2 * blk, SWA_GROUP * blk)


def kernel(x, mem, rel_bias, g_final, g_ffn1, w1_gate, w1_up, w1_down, g_mix, w_in, b_in, sinks, g_out_moba, g_out_swa, w_out, g_cross, g_mem, w_cq, w_ck, w_cv, w_co, g_ffn2, w2_gate, w2_up, w2_down):
    b, s, d = x.shape
    n_mem = mem.shape[1]
    depth = w_in.shape[0]
    assert s % MOBA_BLOCK == 0 and s % SWA_BLOCK == 0 and d % LANES == 0

    bias_moba = rel_bias[:, :N_MOBA_HEADS].T.astype(F32)
    bias_swa = rel_bias[:, N_MOBA_HEADS:].T.astype(F32)
    town, tprev, bfar = _moba_bias_tables(bias_moba)
    tswa = _swa_bias_table(bias_swa)

    xt = x.reshape(b * s, d)
    memt = mem.reshape(b * n_mem, d)
    for l in range(depth):
        xt = _ffn(xt, g_ffn1[l], w1_gate[l].astype(BF16), w1_up[l].astype(BF16), w1_down[l].astype(BF16))

        nrm = _rmsnorm(xt, g_mix[l], BF16)
        proj = _proj_slabs(nrm, w_in[l].astype(BF16), b_in[l]).reshape(N_SLABS, b, s, HEAD_DIM)
        o_m = _moba(proj, town, tprev, bfar).reshape(N_MOBA_HEADS, b * s, HEAD_DIM)
        sink = jnp.broadcast_to(sinks[l].astype(F32).reshape(N_SWA_KV_HEADS, SWA_GROUP, 1),
                                (N_SWA_KV_HEADS, SWA_GROUP, SWA_BLOCK)).reshape(N_SWA_KV_HEADS, 1, SWA_GROUP * SWA_BLOCK)
        o_s = _swa(proj, tswa, sink).reshape(N_SWA_HEADS, b * s, HEAD_DIM)
        o = _head_norm(o_m, o_s, g_out_moba[l], g_out_swa[l])
        xt = _matmul_residual(o, w_out[l].astype(BF16), xt)

        hc = _rmsnorm(xt, g_cross[l], BF16)
        mem_n = _rmsnorm(memt, g_mem[l], BF16)
        w_kv = jnp.concatenate([w_ck[l], w_cv[l]], axis=1).astype(BF16)
        kv = _matmul(mem_n, w_kv, BF16).reshape(b, n_mem, 2 * CROSS_W)
        xt = _cross(hc, w_cq[l].astype(BF16), kv, w_co[l].astype(BF16), xt, s)

        xt = _ffn(xt, g_ffn2[l], w2_gate[l].astype(BF16), w2_up[l].astype(BF16), w2_down[l].astype(BF16))
    return _rmsnorm(xt, g_final, F32).reshape(b, s, d)
```

```python
import functools
import math

import numpy as np
import jax
import jax.numpy as jnp
from jax import lax
from jax.experimental import pallas as pl
from jax.experimental.pallas import tpu as pltpu

HEAD_DIM = 128
N_MOBA_HEADS = 16
N_SWA_HEADS = 16
N_SWA_KV_HEADS = 4
SWA_GROUP = N_SWA_HEADS // N_SWA_KV_HEADS
MOBA_BLOCK = 256
MOBA_TOPK = 3
SWA_WINDOW = 128
SWA_BLOCK = 128
N_CROSS_HEADS = 4
NUM_BUCKETS = 32
MAX_DISTANCE = 128
EPS = 1e-6

MOBA_W = N_MOBA_HEADS * HEAD_DIM
SWA_W = N_SWA_HEADS * HEAD_DIM
SWA_KV_W = N_SWA_KV_HEADS * HEAD_DIM
CROSS_W = N_CROSS_HEADS * HEAD_DIM

SLAB_QM = 0
SLAB_KM = SLAB_QM + N_MOBA_HEADS
SLAB_VM = SLAB_KM + N_MOBA_HEADS
SLAB_QS = SLAB_VM + N_MOBA_HEADS
SLAB_KS = SLAB_QS + N_SWA_HEADS
SLAB_VS = SLAB_KS + N_SWA_KV_HEADS
N_SLABS = SLAB_VS + N_SWA_KV_HEADS

MASKED = -1e30
LANES = 128
V7X_VMEM_LIMIT_BYTES = 56 * 1024 * 1024

F32 = jnp.float32
BF16 = jnp.bfloat16

_NT = (((1,), (1,)), ((), ()))
_TN = (((0,), (0,)), ((), ()))


def _tile(n, want):
    if n <= want:
        return n
    t = want
    while n % t:
        t -= 8
    return t


def _params(*semantics):
    return pltpu.CompilerParams(dimension_semantics=semantics, vmem_limit_bytes=V7X_VMEM_LIMIT_BYTES)


def _rms_rows(x, g):
    ms = jnp.mean(x * x, axis=-1, keepdims=True)
    return x * lax.rsqrt(ms + EPS) * g


def _rmsnorm_body(x_ref, g_ref, o_ref):
    o_ref[...] = _rms_rows(x_ref[...], g_ref[...]).astype(o_ref.dtype)


def _rmsnorm(x, g, out_dtype):
    m, d = x.shape
    tm = _tile(m, 256)
    return pl.pallas_call(
        _rmsnorm_body,
        grid=(m // tm,),
        in_specs=[pl.BlockSpec((tm, d), lambda i: (i, 0)), pl.BlockSpec((1, d), lambda i: (0, 0))],
        out_specs=pl.BlockSpec((tm, d), lambda i: (i, 0)),
        out_shape=jax.ShapeDtypeStruct((m, d), out_dtype),
        compiler_params=_params("parallel"),
        name="rmsnorm",
    )(x, g.reshape(1, d))


FFN_ROW_CHUNK = 32


def _ffn_body(x_ref, g_ref, gout_ref, wgu_hbm, wd_hbm, o_ref, h_ref, wgu_buf, wd_buf, sem, *, n_chunks, final_norm):
    i = pl.program_id(0)
    tm = x_ref.shape[0]
    tf = wd_buf.shape[1]
    last = n_chunks - 1
    backwards = lax.rem(i, 2) == 1

    def fetch(pos, slot):
        j = jnp.where(backwards, last - pos, pos)
        return (pltpu.make_async_copy(wgu_hbm.at[j], wgu_buf.at[slot], sem.at[0, slot]),
                pltpu.make_async_copy(wd_hbm.at[j], wd_buf.at[slot], sem.at[1, slot]))

    def start(pos, slot):
        for copy in fetch(pos, slot):
            copy.start()

    def wait(pos, slot):
        for copy in fetch(pos, slot):
            copy.wait()

    @pl.when(i == 0)
    def _():
        start(0, 0)

    def norm_rows(r, carry):
        rows = pl.ds(pl.multiple_of(r * FFN_ROW_CHUNK, FFN_ROW_CHUNK), FFN_ROW_CHUNK)
        x = x_ref[rows, :]
        h_ref[rows, :] = _rms_rows(x, g_ref[...]).astype(BF16)
        o_ref[rows, :] = x
        return carry

    lax.fori_loop(0, tm // FFN_ROW_CHUNK, norm_rows, 0)

    @pl.when(i == 0)
    def _():
        wait(0, 0)

    def compute(slot):
        gu = jnp.dot(h_ref[...], wgu_buf[slot], preferred_element_type=F32)
        gate = gu[:, :tf]
        up = gu[:, tf:]
        act = (0.5 * gate * jax.nn.sigmoid(gate) * up).astype(BF16)
        o_ref[...] += jnp.dot(act, wd_buf[slot], preferred_element_type=F32)

    def chunk_pair(pp, carry):
        pos = 2 * pp

        @pl.when(pos < last)
        def _():
            start(pos + 1, 1)

        compute(0)

        @pl.when(pos < last)
        def _():
            wait(pos + 1, 1)
            start(pos + 2, 0)
            compute(1)
            wait(pos + 2, 0)

        return carry

    lax.fori_loop(0, (n_chunks + 1) // 2, chunk_pair, 0)

    if final_norm:
        def out_rows(r, carry):
            rows = pl.ds(pl.multiple_of(r * FFN_ROW_CHUNK, FFN_ROW_CHUNK), FFN_ROW_CHUNK)
            o_ref[rows, :] = _rms_rows(o_ref[rows, :], gout_ref[...])
            return carry

        lax.fori_loop(0, tm // FFN_ROW_CHUNK, out_rows, 0)


def _ffn(x, g, wg, wu, wd, g_out=None):
    m, d = x.shape
    f = wg.shape[1]
    tm = _tile(m, 512)
    tf = 256 if f % 256 == 0 else LANES
    n_chunks = f // tf
    assert f % tf == 0 and n_chunks % 2 == 1 and tm % FFN_ROW_CHUNK == 0
    wgu = jnp.concatenate([wg.reshape(d, n_chunks, tf), wu.reshape(d, n_chunks, tf)], axis=2)
    wgu = wgu.transpose(1, 0, 2).astype(BF16)
    wdc = wd.reshape(n_chunks, tf, d).astype(BF16)
    final_norm = g_out is not None
    body = functools.partial(_ffn_body, n_chunks=n_chunks, final_norm=final_norm)
    return pl.pallas_call(
        body,
        grid=(m // tm,),
        in_specs=[
            pl.BlockSpec((tm, d), lambda i: (i, 0)),
            pl.BlockSpec((1, d), lambda i: (0, 0)),
            pl.BlockSpec((1, d), lambda i: (0, 0)),
            pl.BlockSpec(memory_space=pl.ANY),
            pl.BlockSpec(memory_space=pl.ANY),
        ],
        out_specs=pl.BlockSpec((tm, d), lambda i: (i, 0)),
        out_shape=jax.ShapeDtypeStruct((m, d), F32),
        scratch_shapes=[
            pltpu.VMEM((tm, d), BF16),
            pltpu.VMEM((2, d, 2 * tf), BF16),
            pltpu.VMEM((2, tf, d), BF16),
            pltpu.SemaphoreType.DMA((2, 2)),
        ],
        compiler_params=_params("arbitrary"),
        name="ffn",
    )(x, g.reshape(1, d), (g_out if final_norm else g).reshape(1, d), wgu, wdc)


def _proj_slabs_body(a_ref, w_ref, b_ref, o_ref):
    acc = jnp.dot(a_ref[...], w_ref[...], preferred_element_type=F32) + b_ref[...]
    for c in range(o_ref.shape[0]):
        o_ref[c] = acc[:, c * LANES:(c + 1) * LANES].astype(o_ref.dtype)


def _proj_slabs(a, w, b):
    m, k = a.shape
    n = w.shape[1]
    tm = _tile(m, 1024)
    tn = _tile(n, 1024)
    return pl.pallas_call(
        _proj_slabs_body,
        grid=(m // tm, n // tn),
        in_specs=[
            pl.BlockSpec((tm, k), lambda i, j: (i, 0)),
            pl.BlockSpec((k, tn), lambda i, j: (0, j)),
            pl.BlockSpec((1, tn), lambda i, j: (0, j)),
        ],
        out_specs=pl.BlockSpec((tn // LANES, tm, LANES), lambda i, j: (j, i, 0)),
        out_shape=jax.ShapeDtypeStruct((n // LANES, m, LANES), BF16),
        compiler_params=_params("parallel", "parallel"),
        name="in_proj",
    )(a, w, b.reshape(1, n))


def _matmul_body(a_ref, w_ref, o_ref):
    o_ref[...] = jnp.dot(a_ref[...], w_ref[...], preferred_element_type=F32).astype(o_ref.dtype)


def _matmul(a, w, out_dtype):
    m, k = a.shape
    n = w.shape[1]
    tm = _tile(m, 1024)
    tn = _tile(n, 512)
    return pl.pallas_call(
        _matmul_body,
        grid=(m // tm, n // tn),
        in_specs=[pl.BlockSpec((tm, k), lambda i, j: (i, 0)), pl.BlockSpec((k, tn), lambda i, j: (0, j))],
        out_specs=pl.BlockSpec((tm, tn), lambda i, j: (i, j)),
        out_shape=jax.ShapeDtypeStruct((m, n), out_dtype),
        compiler_params=_params("parallel", "parallel"),
        name="matmul",
    )(a, w)


def _matmul_residual_body(a_ref, w_ref, r_ref, o_ref):
    o_ref[...] = r_ref[...] + jnp.dot(a_ref[...], w_ref[...], preferred_element_type=F32)


def _matmul_residual(a, w, res):
    m, k = a.shape
    n = w.shape[1]
    tm = _tile(m, 1024)
    tn = _tile(n, 512)
    return pl.pallas_call(
        _matmul_residual_body,
        grid=(m // tm, n // tn),
        in_specs=[
            pl.BlockSpec((tm, k), lambda i, j: (i, 0)),
            pl.BlockSpec((k, tn), lambda i, j: (0, j)),
            pl.BlockSpec((tm, tn), lambda i, j: (i, j)),
        ],
        out_specs=pl.BlockSpec((tm, tn), lambda i, j: (i, j)),
        out_shape=jax.ShapeDtypeStruct((m, n), F32),
        compiler_params=_params("parallel", "parallel"),
        name="out_proj",
    )(a, w, res)


def _fold_rows(x, op, rows=8):
    while x.shape[0] > rows:
        half = x.shape[0] // 2
        x = op(x[:half], x[half:])
    return x


def _moba_body(q_ref, k_ref, v_ref, town_ref, tprev_ref, bfar_ref, o_ref, km_ref, sel_ref, s_ref, l_ref, acc_ref,
               *, n_blocks, topk, heads):
    n = pl.program_id(2)
    blk = MOBA_BLOCK
    pair = 2 * blk
    scale = HEAD_DIM ** -0.5
    group = range(heads)

    @pl.when(n == 0)
    def _():
        for g in group:
            for j in range(n_blocks):
                kj = k_ref[g, 0, j * blk:(j + 1) * blk, :].astype(F32)
                km_ref[g, j:j + 1, :] = jnp.mean(kj, axis=0, keepdims=True)

    q = [q_ref[g, 0] for g in group]

    for g in group:
        gate = lax.dot_general(km_ref[g].astype(BF16), q[g], _NT, preferred_element_type=F32)
        rows = lax.broadcasted_iota(jnp.int32, gate.shape, 0)
        gate = jnp.where(rows < n, gate, -jnp.inf)
        sel = jnp.zeros(gate.shape, jnp.bool_)
        for _ in range(topk):
            best = jnp.max(gate, axis=0, keepdims=True)
            first = jnp.min(jnp.where(gate == best, rows, n_blocks), axis=0, keepdims=True)
            pick = (rows == first) & (best > -jnp.inf)
            sel = sel | pick
            gate = jnp.where(pick, -jnp.inf, gate)
        sel_ref[g] = jnp.where(sel, 0.0, MASKED)

    def scores(g, rows):
        return lax.dot_general(k_ref[g, 0, rows, :], q[g], _NT, preferred_element_type=F32) * scale

    def pv(g, rows, p):
        return lax.dot_general(v_ref[g, 0, rows, :], p.astype(BF16), _TN, preferred_element_type=F32)

    jp = jnp.maximum(n - 1, 0)
    own_rows = pl.ds(pl.multiple_of(n * blk, blk), blk)
    prev_rows = pl.ds(pl.multiple_of(jp * blk, blk), blk)
    m8 = []
    for g in group:
        s_own = scores(g, own_rows) + town_ref[g]
        s_prev = scores(g, prev_rows) + tprev_ref[g] + sel_ref[g, pl.ds(jp, 1), :]
        s_ref[g, 0:blk, :] = s_own
        s_ref[g, blk:pair, :] = s_prev
        m8.append(jnp.maximum(_fold_rows(s_own, jnp.maximum), _fold_rows(s_prev, jnp.maximum)))

    n_far = n - 1
    n_pairs = lax.shift_right_logical(n, 1)

    def far_scores(c, m8):
        key_rows = pl.ds(pl.multiple_of(c * pair, pair), pair)
        base = pl.multiple_of(pair + c * pair, pair)
        j1 = 2 * c + 1
        out = []
        for g in group:
            s = scores(g, key_rows)
            bfar = bfar_ref[g]
            s0 = s[:blk] + (sel_ref[g, pl.ds(2 * c, 1), :] + bfar)
            s1 = s[blk:] + jnp.where(j1 < n_far, sel_ref[g, pl.ds(j1, 1), :] + bfar, MASKED)
            s_ref[g, pl.ds(base, blk), :] = s0
            s_ref[g, pl.ds(base + blk, blk), :] = s1
            out.append(jnp.maximum(m8[g], jnp.maximum(_fold_rows(s0, jnp.maximum), _fold_rows(s1, jnp.maximum))))
        return tuple(out)

    m8 = lax.fori_loop(0, n_pairs, far_scores, tuple(m8))
    m = [jnp.max(m8[g], axis=0, keepdims=True) for g in group]

    for g in group:
        p_own = jnp.exp(s_ref[g, 0:blk, :] - m[g])
        p_prev = jnp.exp(s_ref[g, blk:pair, :] - m[g])
        l_ref[g] = _fold_rows(p_own, jnp.add) + _fold_rows(p_prev, jnp.add)
        acc_ref[g] = pv(g, own_rows, p_own) + pv(g, prev_rows, p_prev)

    def far_pv(c, carry):
        key_rows = pl.ds(pl.multiple_of(c * pair, pair), pair)
        base = pl.multiple_of(pair + c * pair, pair)
        for g in group:
            p = jnp.exp(s_ref[g, pl.ds(base, pair), :] - m[g])
            l_ref[g] += _fold_rows(p, jnp.add)
            acc_ref[g] += pv(g, key_rows, p)
        return carry

    lax.fori_loop(0, n_pairs, far_pv, 0)
    for g in group:
        l = jnp.sum(l_ref[g], axis=0, keepdims=True)
        o_ref[g, 0] = (acc_ref[g] / l).T.astype(o_ref.dtype)


def _moba(proj, town, tprev, bfar):
    _, b, s, _ = proj.shape
    blk = MOBA_BLOCK
    n_blocks = s // blk
    heads = 4
    body = functools.partial(_moba_body, n_blocks=n_blocks, topk=min(MOBA_TOPK, n_blocks), heads=heads)
    return pl.pallas_call(
        body,
        grid=(b, N_MOBA_HEADS // heads, n_blocks),
        in_specs=[
            pl.BlockSpec((heads, 1, blk, HEAD_DIM), lambda bi, h, n: (SLAB_QM // heads + h, bi, n, 0)),
            pl.BlockSpec((heads, 1, s, HEAD_DIM), lambda bi, h, n: (SLAB_KM // heads + h, bi, 0, 0)),
            pl.BlockSpec((heads, 1, s, HEAD_DIM), lambda bi, h, n: (SLAB_VM // heads + h, bi, 0, 0)),
            pl.BlockSpec((heads, blk, blk), lambda bi, h, n: (h, 0, 0)),
            pl.BlockSpec((heads, blk, blk), lambda bi, h, n: (h, 0, 0)),
            pl.BlockSpec((heads, 1, blk), lambda bi, h, n: (h, 0, 0)),
        ],
        out_specs=pl.BlockSpec((heads, 1, blk, HEAD_DIM), lambda bi, h, n: (h, bi, n, 0)),
        out_shape=jax.ShapeDtypeStruct((N_MOBA_HEADS, b, s, HEAD_DIM), F32),
        scratch_shapes=[
            pltpu.VMEM((heads, n_blocks, HEAD_DIM), F32),
            pltpu.VMEM((heads, n_blocks, blk), F32),
            pltpu.VMEM((heads, 2 * blk * (1 + (n_blocks - 1) // 2), blk), F32),
            pltpu.VMEM((heads, 8, blk), F32),
            pltpu.VMEM((heads, HEAD_DIM, blk), F32),
        ],
        compiler_params=_params("parallel", "parallel", "arbitrary"),
        name="moba",
    )(proj, proj, proj, town, tprev, bfar)


def _swa_body(q_ref, k_ref, v_ref, bias_ref, sink_ref, o_ref, *, blocks_per_step):
    blk = SWA_BLOCK
    scale = HEAD_DIM ** -0.5
    sink = sink_ref[0]
    for r in range(blocks_per_step):
        n = pl.program_id(2) * blocks_per_step + r
        q_rows = slice(r * blk, (r + 1) * blk)
        q = jnp.concatenate([q_ref[g, 0, q_rows, :] for g in range(SWA_GROUP)], axis=0)
        prev_rows = pl.ds(pl.multiple_of(jnp.maximum(n - 1, 0) * blk, blk), blk)
        own_rows = pl.ds(pl.multiple_of(n * blk, blk), blk)
        k = jnp.concatenate([k_ref[0, 0, prev_rows, :], k_ref[0, 0, own_rows, :]], axis=0)
        v = jnp.concatenate([v_ref[0, 0, prev_rows, :], v_ref[0, 0, own_rows, :]], axis=0)
        s = lax.dot_general(k, q, _NT, preferred_element_type=F32) * scale + bias_ref[0]
        key_row = lax.broadcasted_iota(jnp.int32, s.shape, 0)
        s = jnp.where((key_row < blk) & (n == 0), MASKED, s)
        m = jnp.maximum(jnp.max(s, axis=0, keepdims=True), sink)
        p = jnp.exp(s - m)
        l = jnp.sum(p, axis=0, keepdims=True) + jnp.exp(sink - m)
        o = lax.dot_general(v, p.astype(BF16), _TN, preferred_element_type=F32) / l
        for g in range(SWA_GROUP):
            o_ref[g, 0, q_rows, :] = o[:, g * blk:(g + 1) * blk].T.astype(o_ref.dtype)


def _swa(proj, bias, sink):
    _, b, s, _ = proj.shape
    blk = SWA_BLOCK
    gq = SWA_GROUP * blk
    per_step = 4 if (s // blk) % 4 == 0 else 1
    rows = per_step * blk
    return pl.pallas_call(
        functools.partial(_swa_body, blocks_per_step=per_step),
        grid=(b, N_SWA_KV_HEADS, s // rows),
        in_specs=[
            pl.BlockSpec((SWA_GROUP, 1, rows, HEAD_DIM), lambda bi, h, n: (SLAB_QS // SWA_GROUP + h, bi, n, 0)),
            pl.BlockSpec((1, 1, s, HEAD_DIM), lambda bi, h, n: (SLAB_KS + h, bi, 0, 0)),
            pl.BlockSpec((1, 1, s, HEAD_DIM), lambda bi, h, n: (SLAB_VS + h, bi, 0, 0)),
            pl.BlockSpec((1, 2 * blk, gq), lambda bi, h, n: (h, 0, 0)),
            pl.BlockSpec((1, 1, gq), lambda bi, h, n: (h, 0, 0)),
        ],
        out_specs=pl.BlockSpec((SWA_GROUP, 1, rows, HEAD_DIM), lambda bi, h, n: (h, bi, n, 0)),
        out_shape=jax.ShapeDtypeStruct((N_SWA_HEADS, b, s, HEAD_DIM), F32),
        compiler_params=_params("parallel", "parallel", "arbitrary"),
        name="swa",
    )(proj, proj, proj, bias, sink)


def _head_norm_body(om_ref, os_ref, gm_ref, gs_ref, o_ref):
    col = 0
    for x_ref, g_ref in ((om_ref, gm_ref), (os_ref, gs_ref)):
        heads = x_ref.shape[0]
        ss = sum(jnp.sum(jnp.square(x_ref[c]), axis=-1, keepdims=True) for c in range(heads))
        r = lax.rsqrt(ss / (heads * HEAD_DIM) + EPS)
        for c in range(heads):
            o_ref[:, col:col + HEAD_DIM] = (x_ref[c] * r * g_ref[:, c * HEAD_DIM:(c + 1) * HEAD_DIM]).astype(o_ref.dtype)
            col += HEAD_DIM


def _head_norm(om, osw, gm, gs):
    hm, m, _ = om.shape
    hs = osw.shape[0]
    tm = _tile(m, 256)
    width = (hm + hs) * HEAD_DIM
    return pl.pallas_call(
        _head_norm_body,
        grid=(m // tm,),
        in_specs=[
            pl.BlockSpec((hm, tm, HEAD_DIM), lambda i: (0, i, 0)),
            pl.BlockSpec((hs, tm, HEAD_DIM), lambda i: (0, i, 0)),
            pl.BlockSpec((1, hm * HEAD_DIM), lambda i: (0, 0)),
            pl.BlockSpec((1, hs * HEAD_DIM), lambda i: (0, 0)),
        ],
        out_specs=pl.BlockSpec((tm, width), lambda i: (i, 0)),
        out_shape=jax.ShapeDtypeStruct((m, width), BF16),
        compiler_params=_params("parallel"),
        name="head_norm",
    )(om, osw, gm.reshape(1, -1), gs.reshape(1, -1))


def _cross_body(x_ref, g_ref, wq_ref, kv_ref, wo_ref, o_ref, h_ref):
    scale = HEAD_DIM ** -0.5
    tm = x_ref.shape[0]

    def norm_rows(r, carry):
        rows = pl.ds(pl.multiple_of(r * FFN_ROW_CHUNK, FFN_ROW_CHUNK), FFN_ROW_CHUNK)
        h_ref[rows, :] = _rms_rows(x_ref[rows, :], g_ref[...]).astype(BF16)
        return carry

    lax.fori_loop(0, tm // FFN_ROW_CHUNK, norm_rows, 0)
    q = jnp.dot(h_ref[...], wq_ref[...], preferred_element_type=F32).astype(BF16)
    outs = []
    for hd in range(N_CROSS_HEADS):
        cols = slice(hd * HEAD_DIM, (hd + 1) * HEAD_DIM)
        k = kv_ref[0, :, cols]
        v = kv_ref[0, :, CROSS_W + hd * HEAD_DIM:CROSS_W + (hd + 1) * HEAD_DIM]
        s = lax.dot_general(q[:, cols], k, _NT, preferred_element_type=F32) * scale
        p = jnp.exp(s - jnp.max(s, axis=-1, keepdims=True))
        l = jnp.sum(p, axis=-1, keepdims=True)
        outs.append((jnp.dot(p.astype(BF16), v, preferred_element_type=F32) / l).astype(BF16))
    o = jnp.concatenate(outs, axis=-1)
    o_ref[...] = x_ref[...] + jnp.dot(o, wo_ref[...], preferred_element_type=F32)


def _cross(x, g, wq, kv, wo, seq):
    m, d = x.shape
    n_mem = kv.shape[1]
    tm = _tile(seq, 256)
    assert tm % FFN_ROW_CHUNK == 0
    tiles_per_seq = seq // tm
    return pl.pallas_call(
        _cross_body,
        grid=(m // tm,),
        in_specs=[
            pl.BlockSpec((tm, d), lambda i: (i, 0)),
            pl.BlockSpec((1, d), lambda i: (0, 0)),
            pl.BlockSpec((d, CROSS_W), lambda i: (0, 0)),
            pl.BlockSpec((1, n_mem, 2 * CROSS_W), lambda i: (i // tiles_per_seq, 0, 0)),
            pl.BlockSpec((CROSS_W, d), lambda i: (0, 0)),
        ],
        out_specs=pl.BlockSpec((tm, d), lambda i: (i, 0)),
        out_shape=jax.ShapeDtypeStruct((m, d), F32),
        scratch_shapes=[pltpu.VMEM((tm, d), BF16)],
        compiler_params=_params("parallel"),
        name="cross",
    )(x, g.reshape(1, d), wq, kv, wo)


def _t5_bucket(dist):
    n = np.maximum(dist, 0)
    max_exact = NUM_BUCKETS // 2
    nf = np.maximum(n, max_exact).astype(np.float32)
    large = max_exact + (np.log(nf / np.float32(max_exact)) / np.float32(math.log(MAX_DISTANCE / max_exact))
                         * np.float32(NUM_BUCKETS - max_exact)).astype(np.int32)
    large = np.minimum(large, NUM_BUCKETS - 1)
    return np.where(n < max_exact, n, large).astype(np.int32)


def _bias_by_distance(bias_hb, dists):
    onehot = _t5_bucket(dists)[None, :] == np.arange(NUM_BUCKETS)[:, None]
    return jnp.sum(jnp.where(onehot[None], bias_hb[:, :, None], 0.0), axis=1)


def _toeplitz(g, rows, cols):
    heads, n = g.shape
    assert n == rows + cols - 1
    width = rows + cols
    tiled = jnp.broadcast_to(jnp.pad(g, ((0, 0), (0, 1)))[:, None, :], (heads, rows, width))
    skew = tiled.reshape(heads, rows * width)[:, :rows * (width - 1)].reshape(heads, rows, width - 1)
    return skew[:, :, rows - 1:rows - 1 + cols]


def _moba_bias_tables(bias_hb):
    blk = MOBA_BLOCK
    d_own = np.arange(2 * blk - 1) - (blk - 1)
    g_own = jnp.where(d_own >= 0, _bias_by_distance(bias_hb, np.maximum(d_own, 0)), MASKED)
    g_prev = _bias_by_distance(bias_hb, d_own + blk)
    far = int(_t5_bucket(np.array(blk + 1)))
    bfar = jnp.broadcast_to(bias_hb[:, far][:, None, None], (bias_hb.shape[0], 1, blk))
    return _toeplitz(g_own, blk, blk), _toeplitz(g_prev, blk, blk), bfar


def _swa_bias_table(bias_hb):
    blk = SWA_BLOCK
    dist = np.arange(3 * blk - 1) - (blk - 1)
    ok = (dist >= 0) & (dist < SWA_WINDOW)
    g = jnp.where(ok, _bias_by_distance(bias_hb, np.maximum(dist, 0)), MASKED)
    t = _toeplitz(g, 2 * blk, blk)
    t = t.reshape(N_SWA_KV_HEADS, SWA_GROUP, 2 * blk, blk).transpose(0, 2, 1, 3)
    return t.reshape(N_SWA_KV_HEADS, 2 * blk, SWA_GROUP * blk)


def kernel(x, mem, rel_bias, g_final, g_ffn1, w1_gate, w1_up, w1_down, g_mix, w_in, b_in, sinks, g_out_moba, g_out_swa, w_out, g_cross, g_mem, w_cq, w_ck, w_cv, w_co, g_ffn2, w2_gate, w2_up, w2_down):
    b, s, d = x.shape
    n_mem = mem.shape[1]
    depth = w_in.shape[0]
    assert s % MOBA_BLOCK == 0 and s % SWA_BLOCK == 0 and d % LANES == 0

    bias_moba = rel_bias[:, :N_MOBA_HEADS].T.astype(F32)
    bias_swa = rel_bias[:, N_MOBA_HEADS:].T.astype(F32)
    town, tprev, bfar = _moba_bias_tables(bias_moba)
    tswa = _swa_bias_table(bias_swa)

    xt = x.reshape(b * s, d)
    memt = mem.reshape(b * n_mem, d)
    for l in range(depth):
        xt = _ffn(xt, g_ffn1[l], w1_gate[l], w1_up[l], w1_down[l])

        nrm = _rmsnorm(xt, g_mix[l], BF16)
        proj = _proj_slabs(nrm, w_in[l].astype(BF16), b_in[l]).reshape(N_SLABS, b, s, HEAD_DIM)
        o_m = _moba(proj, town, tprev, bfar).reshape(N_MOBA_HEADS, b * s, HEAD_DIM)
        sink = jnp.broadcast_to(sinks[l].astype(F32).reshape(N_SWA_KV_HEADS, SWA_GROUP, 1),
                                (N_SWA_KV_HEADS, SWA_GROUP, SWA_BLOCK)).reshape(N_SWA_KV_HEADS, 1, SWA_GROUP * SWA_BLOCK)
        o_s = _swa(proj, tswa, sink).reshape(N_SWA_HEADS, b * s, HEAD_DIM)
        o = _head_norm(o_m, o_s, g_out_moba[l], g_out_swa[l])
        xt = _matmul_residual(o, w_out[l].astype(BF16), xt)

        mem_n = _rmsnorm(memt, g_mem[l], BF16)
        w_kv = jnp.concatenate([w_ck[l], w_cv[l]], axis=1).astype(BF16)
        kv = _matmul(mem_n, w_kv, BF16).reshape(b, n_mem, 2 * CROSS_W)
        xt = _cross(xt, g_cross[l], w_cq[l].astype(BF16), kv, w_co[l].astype(BF16), s)

        xt = _ffn(xt, g_ffn2[l], w2_gate[l], w2_up[l], w2_down[l], g_out=g_final if l == depth - 1 else None)
    return xt.reshape(b, s, d)
```

```python
import functools
import math

import numpy as np
import jax
import jax.numpy as jnp
from jax import lax
from jax.experimental import pallas as pl
from jax.experimental.pallas import tpu as pltpu

HEAD_DIM = 128
N_MOBA_HEADS = 16
N_SWA_HEADS = 16
N_SWA_KV_HEADS = 4
SWA_GROUP = N_SWA_HEADS // N_SWA_KV_HEADS
MOBA_BLOCK = 256
MOBA_TOPK = 3
SWA_WINDOW = 128
SWA_BLOCK = 128
N_CROSS_HEADS = 4
NUM_BUCKETS = 32
MAX_DISTANCE = 128
EPS = 1e-6

MOBA_W = N_MOBA_HEADS * HEAD_DIM
SWA_W = N_SWA_HEADS * HEAD_DIM
SWA_KV_W = N_SWA_KV_HEADS * HEAD_DIM
CROSS_W = N_CROSS_HEADS * HEAD_DIM

SLAB_QM = 0
SLAB_KM = SLAB_QM + N_MOBA_HEADS
SLAB_VM = SLAB_KM + N_MOBA_HEADS
SLAB_QS = SLAB_VM + N_MOBA_HEADS
SLAB_KS = SLAB_QS + N_SWA_HEADS
SLAB_VS = SLAB_KS + N_SWA_KV_HEADS
N_SLABS = SLAB_VS + N_SWA_KV_HEADS

MASKED = -1e30
LOG2E = math.log2(math.e)
SCORE_SCALE = HEAD_DIM ** -0.5 * LOG2E
LANES = 128
V7X_VMEM_LIMIT_BYTES = 56 * 1024 * 1024

F32 = jnp.float32
BF16 = jnp.bfloat16

_NT = (((1,), (1,)), ((), ()))
_TN = (((0,), (0,)), ((), ()))


def _tile(n, want):
    if n <= want:
        return n
    t = want
    while n % t:
        t -= 8
    return t


def _params(*semantics):
    return pltpu.CompilerParams(dimension_semantics=semantics, vmem_limit_bytes=V7X_VMEM_LIMIT_BYTES)


def _rms_rows(x, g):
    ms = jnp.mean(x * x, axis=-1, keepdims=True)
    return x * lax.rsqrt(ms + EPS) * g


def _rmsnorm_body(x_ref, g_ref, o_ref):
    o_ref[...] = _rms_rows(x_ref[...], g_ref[...]).astype(o_ref.dtype)


def _rmsnorm(x, g, out_dtype):
    m, d = x.shape
    tm = _tile(m, 256)
    return pl.pallas_call(
        _rmsnorm_body,
        grid=(m // tm,),
        in_specs=[pl.BlockSpec((tm, d), lambda i: (i, 0)), pl.BlockSpec((1, d), lambda i: (0, 0))],
        out_specs=pl.BlockSpec((tm, d), lambda i: (i, 0)),
        out_shape=jax.ShapeDtypeStruct((m, d), out_dtype),
        compiler_params=_params("parallel"),
        name="rmsnorm",
    )(x, g.reshape(1, d))


FFN_ROW_CHUNK = 32


def _ffn_body(x_ref, g_ref, gout_ref, wg_hbm, wu_hbm, wd_hbm, o_ref, h_ref, wg_buf, wu_buf, wd_buf, sem, *, n_chunks,
              final_norm):
    i = pl.program_id(0)
    tm = x_ref.shape[0]
    tf = wd_buf.shape[1]
    last = n_chunks - 1
    backwards = lax.rem(i, 2) == 1

    def fetch(pos, slot):
        cols = pl.ds(pl.multiple_of(jnp.where(backwards, last - pos, pos) * tf, tf), tf)
        return (pltpu.make_async_copy(wg_hbm.at[:, cols], wg_buf.at[slot], sem.at[0, slot]),
                pltpu.make_async_copy(wu_hbm.at[:, cols], wu_buf.at[slot], sem.at[1, slot]),
                pltpu.make_async_copy(wd_hbm.at[cols, :], wd_buf.at[slot], sem.at[2, slot]))

    def start(pos, slot):
        for copy in fetch(pos, slot):
            copy.start()

    def wait(pos, slot):
        for copy in fetch(pos, slot):
            copy.wait()

    @pl.when(i == 0)
    def _():
        start(0, 0)

    def norm_rows(r, carry):
        rows = pl.ds(pl.multiple_of(r * FFN_ROW_CHUNK, FFN_ROW_CHUNK), FFN_ROW_CHUNK)
        x = x_ref[rows, :]
        h_ref[rows, :] = _rms_rows(x, g_ref[...]).astype(BF16)
        o_ref[rows, :] = x
        return carry

    lax.fori_loop(0, tm // FFN_ROW_CHUNK, norm_rows, 0)

    @pl.when(i == 0)
    def _():
        wait(0, 0)

    def compute(slot):
        h = h_ref[...]
        gate = jnp.dot(h, wg_buf[slot], preferred_element_type=F32)
        up = jnp.dot(h, wu_buf[slot], preferred_element_type=F32)
        act = (0.5 * gate * jax.nn.sigmoid(gate) * up).astype(BF16)
        o_ref[...] += jnp.dot(act, wd_buf[slot], preferred_element_type=F32)

    def chunk_pair(pp, carry):
        pos = 2 * pp

        @pl.when(pos < last)
        def _():
            start(pos + 1, 1)

        compute(0)

        @pl.when(pos < last)
        def _():
            wait(pos + 1, 1)
            start(pos + 2, 0)
            compute(1)
            wait(pos + 2, 0)

        return carry

    lax.fori_loop(0, (n_chunks + 1) // 2, chunk_pair, 0)

    if final_norm:
        def out_rows(r, carry):
            rows = pl.ds(pl.multiple_of(r * FFN_ROW_CHUNK, FFN_ROW_CHUNK), FFN_ROW_CHUNK)
            o_ref[rows, :] = _rms_rows(o_ref[rows, :], gout_ref[...])
            return carry

        lax.fori_loop(0, tm // FFN_ROW_CHUNK, out_rows, 0)


def _ffn(x, g, wg, wu, wd, g_out=None):
    m, d = x.shape
    f = wg.shape[1]
    tm = _tile(m, 512)
    tf = 256 if f % 256 == 0 else LANES
    n_chunks = f // tf
    assert f % tf == 0 and n_chunks % 2 == 1 and tm % FFN_ROW_CHUNK == 0
    final_norm = g_out is not None
    body = functools.partial(_ffn_body, n_chunks=n_chunks, final_norm=final_norm)
    return pl.pallas_call(
        body,
        grid=(m // tm,),
        in_specs=[
            pl.BlockSpec((tm, d), lambda i: (i, 0)),
            pl.BlockSpec((1, d), lambda i: (0, 0)),
            pl.BlockSpec((1, d), lambda i: (0, 0)),
            pl.BlockSpec(memory_space=pl.ANY),
            pl.BlockSpec(memory_space=pl.ANY),
            pl.BlockSpec(memory_space=pl.ANY),
        ],
        out_specs=pl.BlockSpec((tm, d), lambda i: (i, 0)),
        out_shape=jax.ShapeDtypeStruct((m, d), F32),
        scratch_shapes=[
            pltpu.VMEM((tm, d), BF16),
            pltpu.VMEM((2, d, tf), BF16),
            pltpu.VMEM((2, d, tf), BF16),
            pltpu.VMEM((2, tf, d), BF16),
            pltpu.SemaphoreType.DMA((3, 2)),
        ],
        compiler_params=_params("arbitrary"),
        name="ffn",
    )(x, g.reshape(1, d), (g_out if final_norm else g).reshape(1, d), wg.astype(BF16), wu.astype(BF16), wd.astype(BF16))


def _proj_slabs_body(a_ref, w_ref, b_ref, o_ref):
    acc = jnp.dot(a_ref[...], w_ref[...], preferred_element_type=F32) + b_ref[...]
    for c in range(o_ref.shape[0]):
        o_ref[c] = acc[:, c * LANES:(c + 1) * LANES].astype(o_ref.dtype)


def _proj_slabs(a, w, b):
    m, k = a.shape
    n = w.shape[1]
    tm = _tile(m, 1024)
    tn = _tile(n, 1024)
    return pl.pallas_call(
        _proj_slabs_body,
        grid=(m // tm, n // tn),
        in_specs=[
            pl.BlockSpec((tm, k), lambda i, j: (i, 0)),
            pl.BlockSpec((k, tn), lambda i, j: (0, j)),
            pl.BlockSpec((1, tn), lambda i, j: (0, j)),
        ],
        out_specs=pl.BlockSpec((tn // LANES, tm, LANES), lambda i, j: (j, i, 0)),
        out_shape=jax.ShapeDtypeStruct((n // LANES, m, LANES), BF16),
        compiler_params=_params("parallel", "parallel"),
        name="in_proj",
    )(a, w, b.reshape(1, n))


def _matmul_body(a_ref, w_ref, o_ref):
    o_ref[...] = jnp.dot(a_ref[...], w_ref[...], preferred_element_type=F32).astype(o_ref.dtype)


def _matmul(a, w, out_dtype):
    m, k = a.shape
    n = w.shape[1]
    tm = _tile(m, 1024)
    tn = _tile(n, 512)
    return pl.pallas_call(
        _matmul_body,
        grid=(m // tm, n // tn),
        in_specs=[pl.BlockSpec((tm, k), lambda i, j: (i, 0)), pl.BlockSpec((k, tn), lambda i, j: (0, j))],
        out_specs=pl.BlockSpec((tm, tn), lambda i, j: (i, j)),
        out_shape=jax.ShapeDtypeStruct((m, n), out_dtype),
        compiler_params=_params("parallel", "parallel"),
        name="matmul",
    )(a, w)


def _matmul_residual_body(a_ref, w_ref, r_ref, o_ref):
    o_ref[...] = r_ref[...] + jnp.dot(a_ref[...], w_ref[...], preferred_element_type=F32)


def _matmul_residual(a, w, res):
    m, k = a.shape
    n = w.shape[1]
    tm = _tile(m, 1024)
    tn = _tile(n, 512)
    return pl.pallas_call(
        _matmul_residual_body,
        grid=(m // tm, n // tn),
        in_specs=[
            pl.BlockSpec((tm, k), lambda i, j: (i, 0)),
            pl.BlockSpec((k, tn), lambda i, j: (0, j)),
            pl.BlockSpec((tm, tn), lambda i, j: (i, j)),
        ],
        out_specs=pl.BlockSpec((tm, tn), lambda i, j: (i, j)),
        out_shape=jax.ShapeDtypeStruct((m, n), F32),
        compiler_params=_params("parallel", "parallel"),
        name="out_proj",
    )(a, w, res)


def _fold_rows(x, op, rows=8):
    while x.shape[0] > rows:
        half = x.shape[0] // 2
        x = op(x[:half], x[half:])
    return x


def _moba_body(q_ref, k_ref, v_ref, town_ref, tprev_ref, bfar_ref, o_ref, km_ref, sel_ref, s_ref, l_ref, acc_ref,
               *, n_blocks, topk, heads):
    n = pl.program_id(2)
    blk = MOBA_BLOCK
    pair = 2 * blk
    scale = SCORE_SCALE
    group = range(heads)

    @pl.when(n == 0)
    def _():
        for g in group:
            for j in range(n_blocks):
                kj = k_ref[g, 0, j * blk:(j + 1) * blk, :].astype(F32)
                km_ref[g, j:j + 1, :] = jnp.mean(kj, axis=0, keepdims=True)

    q = [q_ref[g, 0] for g in group]

    for g in group:
        gate = lax.dot_general(km_ref[g].astype(BF16), q[g], _NT, preferred_element_type=F32)
        rows = lax.broadcasted_iota(jnp.int32, gate.shape, 0)
        gate = jnp.where(rows < n, gate, -jnp.inf)
        sel = jnp.zeros(gate.shape, jnp.bool_)
        for _ in range(topk):
            best = jnp.max(gate, axis=0, keepdims=True)
            first = jnp.min(jnp.where(gate == best, rows, n_blocks), axis=0, keepdims=True)
            pick = (rows == first) & (best > -jnp.inf)
            sel = sel | pick
            gate = jnp.where(pick, -jnp.inf, gate)
        sel_ref[g] = jnp.where(sel, 0.0, MASKED)

    def scores(g, rows):
        return lax.dot_general(k_ref[g, 0, rows, :], q[g], _NT, preferred_element_type=F32) * scale

    def pv(g, rows, p):
        return lax.dot_general(v_ref[g, 0, rows, :], p.astype(BF16), _TN, preferred_element_type=F32)

    jp = jnp.maximum(n - 1, 0)
    own_rows = pl.ds(pl.multiple_of(n * blk, blk), blk)
    prev_rows = pl.ds(pl.multiple_of(jp * blk, blk), blk)
    m8 = []
    for g in group:
        s_own = scores(g, own_rows) + town_ref[g]
        s_prev = scores(g, prev_rows) + tprev_ref[g] + sel_ref[g, pl.ds(jp, 1), :]
        s_ref[g, 0:blk, :] = s_own
        s_ref[g, blk:pair, :] = s_prev
        m8.append(jnp.maximum(_fold_rows(s_own, jnp.maximum), _fold_rows(s_prev, jnp.maximum)))

    n_far = n - 1
    n_pairs = lax.shift_right_logical(n, 1)

    def far_scores(c, m8):
        key_rows = pl.ds(pl.multiple_of(c * pair, pair), pair)
        base = pl.multiple_of(pair + c * pair, pair)
        j1 = 2 * c + 1
        out = []
        for g in group:
            s = scores(g, key_rows)
            bfar = bfar_ref[g]
            s0 = s[:blk] + (sel_ref[g, pl.ds(2 * c, 1), :] + bfar)
            s1 = s[blk:] + jnp.where(j1 < n_far, sel_ref[g, pl.ds(j1, 1), :] + bfar, MASKED)
            s_ref[g, pl.ds(base, blk), :] = s0
            s_ref[g, pl.ds(base + blk, blk), :] = s1
            out.append(jnp.maximum(m8[g], jnp.maximum(_fold_rows(s0, jnp.maximum), _fold_rows(s1, jnp.maximum))))
        return tuple(out)

    m8 = lax.fori_loop(0, n_pairs, far_scores, tuple(m8))
    m = [jnp.max(m8[g], axis=0, keepdims=True) for g in group]

    for g in group:
        p_own = jnp.exp2(s_ref[g, 0:blk, :] - m[g])
        p_prev = jnp.exp2(s_ref[g, blk:pair, :] - m[g])
        l_ref[g] = _fold_rows(p_own, jnp.add) + _fold_rows(p_prev, jnp.add)
        acc_ref[g] = pv(g, own_rows, p_own) + pv(g, prev_rows, p_prev)

    def far_pv(c, carry):
        key_rows = pl.ds(pl.multiple_of(c * pair, pair), pair)
        base = pl.multiple_of(pair + c * pair, pair)
        for g in group:
            p = jnp.exp2(s_ref[g, pl.ds(base, pair), :] - m[g])
            l_ref[g] += _fold_rows(p, jnp.add)
            acc_ref[g] += pv(g, key_rows, p)
        return carry

    lax.fori_loop(0, n_pairs, far_pv, 0)
    for g in group:
        l = jnp.sum(l_ref[g], axis=0, keepdims=True)
        o_ref[g, 0] = (acc_ref[g] / l).T.astype(o_ref.dtype)


def _moba(proj, town, tprev, bfar):
    _, b, s, _ = proj.shape
    blk = MOBA_BLOCK
    n_blocks = s // blk
    heads = 4
    body = functools.partial(_moba_body, n_blocks=n_blocks, topk=min(MOBA_TOPK, n_blocks), heads=heads)
    return pl.pallas_call(
        body,
        grid=(b, N_MOBA_HEADS // heads, n_blocks),
        in_specs=[
            pl.BlockSpec((heads, 1, blk, HEAD_DIM), lambda bi, h, n: (SLAB_QM // heads + h, bi, n, 0)),
            pl.BlockSpec((heads, 1, s, HEAD_DIM), lambda bi, h, n: (SLAB_KM // heads + h, bi, 0, 0)),
            pl.BlockSpec((heads, 1, s, HEAD_DIM), lambda bi, h, n: (SLAB_VM // heads + h, bi, 0, 0)),
            pl.BlockSpec((heads, blk, blk), lambda bi, h, n: (h, 0, 0)),
            pl.BlockSpec((heads, blk, blk), lambda bi, h, n: (h, 0, 0)),
            pl.BlockSpec((heads, 1, blk), lambda bi, h, n: (h, 0, 0)),
        ],
        out_specs=pl.BlockSpec((heads, 1, blk, HEAD_DIM), lambda bi, h, n: (h, bi, n, 0)),
        out_shape=jax.ShapeDtypeStruct((N_MOBA_HEADS, b, s, HEAD_DIM), F32),
        scratch_shapes=[
            pltpu.VMEM((heads, n_blocks, HEAD_DIM), F32),
            pltpu.VMEM((heads, n_blocks, blk), F32),
            pltpu.VMEM((heads, 2 * blk * (1 + (n_blocks - 1) // 2), blk), F32),
            pltpu.VMEM((heads, 8, blk), F32),
            pltpu.VMEM((heads, HEAD_DIM, blk), F32),
        ],
        compiler_params=_params("parallel", "parallel", "arbitrary"),
        name="moba",
    )(proj, proj, proj, town, tprev, bfar)


def _swa_body(q_ref, k_ref, v_ref, bias_ref, sink_ref, o_ref, *, blocks_per_step):
    blk = SWA_BLOCK
    scale = SCORE_SCALE
    sink = sink_ref[0]
    for r in range(blocks_per_step):
        n = pl.program_id(2) * blocks_per_step + r
        q_rows = slice(r * blk, (r + 1) * blk)
        q = jnp.concatenate([q_ref[g, 0, q_rows, :] for g in range(SWA_GROUP)], axis=0)
        prev_rows = pl.ds(pl.multiple_of(jnp.maximum(n - 1, 0) * blk, blk), blk)
        own_rows = pl.ds(pl.multiple_of(n * blk, blk), blk)
        k = jnp.concatenate([k_ref[0, 0, prev_rows, :], k_ref[0, 0, own_rows, :]], axis=0)
        v = jnp.concatenate([v_ref[0, 0, prev_rows, :], v_ref[0, 0, own_rows, :]], axis=0)
        s = lax.dot_general(k, q, _NT, preferred_element_type=F32) * scale + bias_ref[0]
        key_row = lax.broadcasted_iota(jnp.int32, s.shape, 0)
        s = jnp.where((key_row < blk) & (n == 0), MASKED, s)
        m = jnp.maximum(jnp.max(s, axis=0, keepdims=True), sink)
        p = jnp.exp2(s - m)
        l = jnp.sum(p, axis=0, keepdims=True) + jnp.exp2(sink - m)
        o = lax.dot_general(v, p.astype(BF16), _TN, preferred_element_type=F32) / l
        for g in range(SWA_GROUP):
            o_ref[g, 0, q_rows, :] = o[:, g * blk:(g + 1) * blk].T.astype(o_ref.dtype)


def _swa(proj, bias, sink):
    _, b, s, _ = proj.shape
    blk = SWA_BLOCK
    gq = SWA_GROUP * blk
    per_step = 4 if (s // blk) % 4 == 0 else 1
    rows = per_step * blk
    return pl.pallas_call(
        functools.partial(_swa_body, blocks_per_step=per_step),
        grid=(b, N_SWA_KV_HEADS, s // rows),
        in_specs=[
            pl.BlockSpec((SWA_GROUP, 1, rows, HEAD_DIM), lambda bi, h, n: (SLAB_QS // SWA_GROUP + h, bi, n, 0)),
            pl.BlockSpec((1, 1, s, HEAD_DIM), lambda bi, h, n: (SLAB_KS + h, bi, 0, 0)),
            pl.BlockSpec((1, 1, s, HEAD_DIM), lambda bi, h, n: (SLAB_VS + h, bi, 0, 0)),
            pl.BlockSpec((1, 2 * blk, gq), lambda bi, h, n: (h, 0, 0)),
            pl.BlockSpec((1, 1, gq), lambda bi, h, n: (h, 0, 0)),
        ],
        out_specs=pl.BlockSpec((SWA_GROUP, 1, rows, HEAD_DIM), lambda bi, h, n: (h, bi, n, 0)),
        out_shape=jax.ShapeDtypeStruct((N_SWA_HEADS, b, s, HEAD_DIM), F32),
        compiler_params=_params("parallel", "parallel", "arbitrary"),
        name="swa",
    )(proj, proj, proj, bias, sink)


def _head_norm_body(om_ref, os_ref, gm_ref, gs_ref, o_ref):
    col = 0
    for x_ref, g_ref in ((om_ref, gm_ref), (os_ref, gs_ref)):
        heads = x_ref.shape[0]
        ss = sum(jnp.sum(jnp.square(x_ref[c]), axis=-1, keepdims=True) for c in range(heads))
        r = lax.rsqrt(ss / (heads * HEAD_DIM) + EPS)
        for c in range(heads):
            o_ref[:, col:col + HEAD_DIM] = (x_ref[c] * r * g_ref[:, c * HEAD_DIM:(c + 1) * HEAD_DIM]).astype(o_ref.dtype)
            col += HEAD_DIM


def _head_norm(om, osw, gm, gs):
    hm, m, _ = om.shape
    hs = osw.shape[0]
    tm = _tile(m, 256)
    width = (hm + hs) * HEAD_DIM
    return pl.pallas_call(
        _head_norm_body,
        grid=(m // tm,),
        in_specs=[
            pl.BlockSpec((hm, tm, HEAD_DIM), lambda i: (0, i, 0)),
            pl.BlockSpec((hs, tm, HEAD_DIM), lambda i: (0, i, 0)),
            pl.BlockSpec((1, hm * HEAD_DIM), lambda i: (0, 0)),
            pl.BlockSpec((1, hs * HEAD_DIM), lambda i: (0, 0)),
        ],
        out_specs=pl.BlockSpec((tm, width), lambda i: (i, 0)),
        out_shape=jax.ShapeDtypeStruct((m, width), BF16),
        compiler_params=_params("parallel"),
        name="head_norm",
    )(om, osw, gm.reshape(1, -1), gs.reshape(1, -1))


def _cross_body(x_ref, g_ref, wq_ref, kv_ref, wo_ref, o_ref, h_ref):
    scale = SCORE_SCALE
    tm = x_ref.shape[0]

    def norm_rows(r, carry):
        rows = pl.ds(pl.multiple_of(r * FFN_ROW_CHUNK, FFN_ROW_CHUNK), FFN_ROW_CHUNK)
        h_ref[rows, :] = _rms_rows(x_ref[rows, :], g_ref[...]).astype(BF16)
        return carry

    lax.fori_loop(0, tm // FFN_ROW_CHUNK, norm_rows, 0)
    q = jnp.dot(h_ref[...], wq_ref[...], preferred_element_type=F32).astype(BF16)
    outs = []
    for hd in range(N_CROSS_HEADS):
        cols = slice(hd * HEAD_DIM, (hd + 1) * HEAD_DIM)
        k = kv_ref[0, :, cols]
        v = kv_ref[0, :, CROSS_W + hd * HEAD_DIM:CROSS_W + (hd + 1) * HEAD_DIM]
        s = lax.dot_general(q[:, cols], k, _NT, preferred_element_type=F32) * scale
        p = jnp.exp2(s - jnp.max(s, axis=-1, keepdims=True))
        l = jnp.sum(p, axis=-1, keepdims=True)
        outs.append((jnp.dot(p.astype(BF16), v, preferred_element_type=F32) / l).astype(BF16))
    o = jnp.concatenate(outs, axis=-1)
    o_ref[...] = x_ref[...] + jnp.dot(o, wo_ref[...], preferred_element_type=F32)


def _cross(x, g, wq, kv, wo, seq):
    m, d = x.shape
    n_mem = kv.shape[1]
    tm = _tile(seq, 512)
    assert tm % FFN_ROW_CHUNK == 0
    tiles_per_seq = seq // tm
    resident = pl.Buffered(1)
    return pl.pallas_call(
        _cross_body,
        grid=(m // tm,),
        in_specs=[
            pl.BlockSpec((tm, d), lambda i: (i, 0)),
            pl.BlockSpec((1, d), lambda i: (0, 0)),
            pl.BlockSpec((d, CROSS_W), lambda i: (0, 0), pipeline_mode=resident),
            pl.BlockSpec((1, n_mem, 2 * CROSS_W), lambda i: (i // tiles_per_seq, 0, 0)),
            pl.BlockSpec((CROSS_W, d), lambda i: (0, 0), pipeline_mode=resident),
        ],
        out_specs=pl.BlockSpec((tm, d), lambda i: (i, 0)),
        out_shape=jax.ShapeDtypeStruct((m, d), F32),
        scratch_shapes=[pltpu.VMEM((tm, d), BF16)],
        compiler_params=_params("parallel"),
        name="cross",
    )(x, g.reshape(1, d), wq, kv, wo)


def _t5_bucket(dist):
    n = np.maximum(dist, 0)
    max_exact = NUM_BUCKETS // 2
    nf = np.maximum(n, max_exact).astype(np.float32)
    large = max_exact + (np.log(nf / np.float32(max_exact)) / np.float32(math.log(MAX_DISTANCE / max_exact))
                         * np.float32(NUM_BUCKETS - max_exact)).astype(np.int32)
    large = np.minimum(large, NUM_BUCKETS - 1)
    return np.where(n < max_exact, n, large).astype(np.int32)


def _bias_by_distance(bias_hb, dists):
    onehot = _t5_bucket(dists)[None, :] == np.arange(NUM_BUCKETS)[:, None]
    return jnp.sum(jnp.where(onehot[None], bias_hb[:, :, None], 0.0), axis=1)


def _toeplitz(g, rows, cols):
    heads, n = g.shape
    assert n == rows + cols - 1
    width = rows + cols
    tiled = jnp.broadcast_to(jnp.pad(g, ((0, 0), (0, 1)))[:, None, :], (heads, rows, width))
    skew = tiled.reshape(heads, rows * width)[:, :rows * (width - 1)].reshape(heads, rows, width - 1)
    return skew[:, :, rows - 1:rows - 1 + cols]


def _moba_bias_tables(bias_hb):
    blk = MOBA_BLOCK
    d_own = np.arange(2 * blk - 1) - (blk - 1)
    g_own = jnp.where(d_own >= 0, _bias_by_distance(bias_hb, np.maximum(d_own, 0)), MASKED)
    g_prev = _bias_by_distance(bias_hb, d_own + blk)
    far = int(_t5_bucket(np.array(blk + 1)))
    bfar = jnp.broadcast_to(bias_hb[:, far][:, None, None], (bias_hb.shape[0], 1, blk))
    return _toeplitz(g_own, blk, blk), _toeplitz(g_prev, blk, blk), bfar


def _swa_bias_table(bias_hb):
    blk = SWA_BLOCK
    dist = np.arange(3 * blk - 1) - (blk - 1)
    ok = (dist >= 0) & (dist < SWA_WINDOW)
    g = jnp.where(ok, _bias_by_distance(bias_hb, np.maximum(dist, 0)), MASKED)
    t = _toeplitz(g, 2 * blk, blk)
    t = t.reshape(N_SWA_KV_HEADS, SWA_GROUP, 2 * blk, blk).transpose(0, 2, 1, 3)
    return t.reshape(N_SWA_KV_HEADS, 2 * blk, SWA_GROUP * blk)


def kernel(x, mem, rel_bias, g_final, g_ffn1, w1_gate, w1_up, w1_down, g_mix, w_in, b_in, sinks, g_out_moba, g_out_swa, w_out, g_cross, g_mem, w_cq, w_ck, w_cv, w_co, g_ffn2, w2_gate, w2_up, w2_down):
    b, s, d = x.shape
    n_mem = mem.shape[1]
    depth = w_in.shape[0]
    assert s % MOBA_BLOCK == 0 and s % SWA_BLOCK == 0 and d % LANES == 0

    bias_moba = rel_bias[:, :N_MOBA_HEADS].T.astype(F32)
    bias_swa = rel_bias[:, N_MOBA_HEADS:].T.astype(F32)
    town, tprev, bfar = _moba_bias_tables(bias_moba * LOG2E)
    tswa = _swa_bias_table(bias_swa * LOG2E)

    xt = x.reshape(b * s, d)
    memt = mem.reshape(b * n_mem, d)
    for l in range(depth):
        xt = _ffn(xt, g_ffn1[l], w1_gate[l], w1_up[l], w1_down[l])

        nrm = _rmsnorm(xt, g_mix[l], BF16)
        proj = _proj_slabs(nrm, w_in[l].astype(BF16), b_in[l]).reshape(N_SLABS, b, s, HEAD_DIM)
        o_m = _moba(proj, town, tprev, bfar).reshape(N_MOBA_HEADS, b * s, HEAD_DIM)
        sink = jnp.broadcast_to((sinks[l].astype(F32) * LOG2E).reshape(N_SWA_KV_HEADS, SWA_GROUP, 1),
                                (N_SWA_KV_HEADS, SWA_GROUP, SWA_BLOCK)).reshape(N_SWA_KV_HEADS, 1, SWA_GROUP * SWA_BLOCK)
        o_s = _swa(proj, tswa, sink).reshape(N_SWA_HEADS, b * s, HEAD_DIM)
        o = _head_norm(o_m, o_s, g_out_moba[l], g_out_swa[l])
        xt = _matmul_residual(o, w_out[l].astype(BF16), xt)

        mem_n = _rmsnorm(memt, g_mem[l], BF16)
        w_kv = jnp.concatenate([w_ck[l], w_cv[l]], axis=1).astype(BF16)
        kv = _matmul(mem_n, w_kv, BF16).reshape(b, n_mem, 2 * CROSS_W)
        xt = _cross(xt, g_cross[l], w_cq[l].astype(BF16), kv, w_co[l].astype(BF16), s)

        xt = _ffn(xt, g_ffn2[l], w2_gate[l], w2_up[l], w2_down[l], g_out=g_final if l == depth - 1 else None)
    return xt.reshape(b, s, d)
```

```python
import functools
import math

import numpy as np
import jax
import jax.numpy as jnp
from jax import lax
from jax.experimental import pallas as pl
from jax.experimental.pallas import tpu as pltpu

HEAD_DIM = 128
N_MOBA_HEADS = 16
N_SWA_HEADS = 16
N_SWA_KV_HEADS = 4
SWA_GROUP = N_SWA_HEADS // N_SWA_KV_HEADS
MOBA_BLOCK = 256
MOBA_TOPK = 3
SWA_WINDOW = 128
SWA_BLOCK = 128
N_CROSS_HEADS = 4
NUM_BUCKETS = 32
MAX_DISTANCE = 128
EPS = 1e-6

MOBA_W = N_MOBA_HEADS * HEAD_DIM
SWA_W = N_SWA_HEADS * HEAD_DIM
SWA_KV_W = N_SWA_KV_HEADS * HEAD_DIM
CROSS_W = N_CROSS_HEADS * HEAD_DIM

SLAB_QM = 0
SLAB_KM = SLAB_QM + N_MOBA_HEADS
SLAB_VM = SLAB_KM + N_MOBA_HEADS
SLAB_QS = SLAB_VM + N_MOBA_HEADS
SLAB_KS = SLAB_QS + N_SWA_HEADS
SLAB_VS = SLAB_KS + N_SWA_KV_HEADS
N_SLABS = SLAB_VS + N_SWA_KV_HEADS

MASKED = -1e30
LOG2E = math.log2(math.e)
SCORE_SCALE = HEAD_DIM ** -0.5 * LOG2E
LANES = 128
V7X_VMEM_LIMIT_BYTES = 56 * 1024 * 1024

F32 = jnp.float32
BF16 = jnp.bfloat16

_NT = (((1,), (1,)), ((), ()))
_TN = (((0,), (0,)), ((), ()))


def _tile(n, want):
    if n <= want:
        return n
    t = want
    while n % t:
        t -= 8
    return t


def _params(*semantics):
    return pltpu.CompilerParams(dimension_semantics=semantics, vmem_limit_bytes=V7X_VMEM_LIMIT_BYTES)


def _rms_rows(x, g):
    ms = jnp.mean(x * x, axis=-1, keepdims=True)
    return x * lax.rsqrt(ms + EPS) * g


def _rmsnorm_body(x_ref, g_ref, o_ref):
    o_ref[...] = _rms_rows(x_ref[...], g_ref[...]).astype(o_ref.dtype)


def _rmsnorm(x, g, out_dtype):
    m, d = x.shape
    tm = _tile(m, 256)
    return pl.pallas_call(
        _rmsnorm_body,
        grid=(m // tm,),
        in_specs=[pl.BlockSpec((tm, d), lambda i: (i, 0)), pl.BlockSpec((1, d), lambda i: (0, 0))],
        out_specs=pl.BlockSpec((tm, d), lambda i: (i, 0)),
        out_shape=jax.ShapeDtypeStruct((m, d), out_dtype),
        compiler_params=_params("parallel"),
        name="rmsnorm",
    )(x, g.reshape(1, d))


FFN_ROW_CHUNK = 64


def _ffn_body(*refs, n_chunks, final_norm, side_cast):
    x_ref, g_ref, gout_ref, wg_hbm, wu_hbm, wd_hbm = refs[:6]
    refs = refs[6:]
    if side_cast:
        src_hbm, refs = refs[:3], refs[3:]
    o_ref, refs = refs[0], refs[1:]
    if side_cast:
        dst_hbm, refs = refs[:3], refs[3:]
    h_ref, wg_buf, wu_buf, wd_buf, sem = refs[:5]
    if side_cast:
        cin, cout, cin_sem, cout_sem = refs[5:8], refs[8:11], refs[11], refs[12]
    i = pl.program_id(0)
    tm = x_ref.shape[0]
    tf = wd_buf.shape[1]
    last = n_chunks - 1
    backwards = lax.rem(i, 2) == 1

    def fetch(pos, slot):
        cols = pl.ds(pl.multiple_of(jnp.where(backwards, last - pos, pos) * tf, tf), tf)
        return (pltpu.make_async_copy(wg_hbm.at[:, cols], wg_buf.at[slot], sem.at[0, slot]),
                pltpu.make_async_copy(wu_hbm.at[:, cols], wu_buf.at[slot], sem.at[1, slot]),
                pltpu.make_async_copy(wd_hbm.at[cols, :], wd_buf.at[slot], sem.at[2, slot]))

    def start(pos, slot):
        for copy in fetch(pos, slot):
            copy.start()

    def wait(pos, slot):
        for copy in fetch(pos, slot):
            copy.wait()

    def pieces(hbm, pos):
        span = cin[0].shape[1]
        own = pl.ds(pl.multiple_of(i * span, span), span)
        chunk = pl.ds(pl.multiple_of(pos * tf, tf), tf)
        return hbm[0].at[own, chunk], hbm[1].at[own, chunk], hbm[2].at[chunk, own]

    def cast_in(pos, slot):
        return [pltpu.make_async_copy(src, cin[k].at[slot], cin_sem.at[k, slot])
                for k, src in enumerate(pieces(src_hbm, pos))]

    def cast_out(pos, slot):
        return [pltpu.make_async_copy(cout[k].at[slot], dst, cout_sem.at[k, slot])
                for k, dst in enumerate(pieces(dst_hbm, pos))]

    def side(copies, method, when=None):
        if not side_cast:
            return

        def go():
            for copy in copies():
                getattr(copy, method)()

        if when is None:
            go()
        else:
            pl.when(when)(go)

    @pl.when(i == 0)
    def _():
        start(0, 0)

    side(lambda: cast_in(0, 0), "start")

    def norm_rows(r, carry):
        rows = pl.ds(pl.multiple_of(r * FFN_ROW_CHUNK, FFN_ROW_CHUNK), FFN_ROW_CHUNK)
        x = x_ref[rows, :]
        o_ref[rows, :] = x
        inv = lax.rsqrt(jnp.mean(x * x, axis=-1, keepdims=True) + EPS)
        h_ref[rows, :] = (o_ref[rows, :] * inv * g_ref[...]).astype(BF16)
        return carry

    lax.fori_loop(0, tm // FFN_ROW_CHUNK, norm_rows, 0)

    @pl.when(i == 0)
    def _():
        wait(0, 0)

    def compute(slot):
        if side_cast:
            for k in range(3):
                cout[k][slot] = cin[k][slot].astype(BF16)
        h = h_ref[...]
        gate = jnp.dot(h, wg_buf[slot], preferred_element_type=F32)
        up = jnp.dot(h, wu_buf[slot], preferred_element_type=F32)
        act = (0.5 * gate * jax.nn.sigmoid(gate) * up).astype(BF16)
        o_ref[...] += jnp.dot(act, wd_buf[slot], preferred_element_type=F32)

    def chunk_pair(pp, carry):
        pos = 2 * pp
        side(lambda: cast_in(pos, 0), "wait")
        side(lambda: cast_out(pos - 1, 1), "start", pos >= 1)
        side(lambda: cast_out(pos - 2, 0), "wait", pos >= 2)

        @pl.when(pos < last)
        def _():
            start(pos + 1, 1)
            side(lambda: cast_in(pos + 1, 1), "start")

        compute(0)

        @pl.when(pos < last)
        def _():
            wait(pos + 1, 1)
            side(lambda: cast_in(pos + 1, 1), "wait")
            side(lambda: cast_out(pos, 0), "start")
            side(lambda: cast_out(pos - 1, 1), "wait", pos >= 1)
            start(pos + 2, 0)
            side(lambda: cast_in(pos + 2, 0), "start")
            compute(1)
            wait(pos + 2, 0)

        return carry

    lax.fori_loop(0, (n_chunks + 1) // 2, chunk_pair, 0)
    side(lambda: cast_out(last, 0), "start")
    side(lambda: cast_out(last - 1, 1), "wait")
    side(lambda: cast_out(last, 0), "wait")

    if final_norm:
        def out_rows(r, carry):
            rows = pl.ds(pl.multiple_of(r * FFN_ROW_CHUNK, FFN_ROW_CHUNK), FFN_ROW_CHUNK)
            o_ref[rows, :] = _rms_rows(o_ref[rows, :], gout_ref[...])
            return carry

        lax.fori_loop(0, tm // FFN_ROW_CHUNK, out_rows, 0)


def _ffn_tiles(m, d, f):
    tm = _tile(m, 512)
    tf = 256 if f % 256 == 0 else LANES
    return tm, tf


def _can_side_cast(m, d, f):
    tm, tf = _ffn_tiles(m, d, f)
    tiles = m // tm
    return d % tiles == 0 and (d // tiles) % LANES == 0


def _ffn(x, g, wg, wu, wd, g_out=None, cast=None):
    m, d = x.shape
    f = wg.shape[1]
    tm, tf = _ffn_tiles(m, d, f)
    n_chunks = f // tf
    tiles = m // tm
    assert f % tf == 0 and n_chunks % 2 == 1 and tm % FFN_ROW_CHUNK == 0
    final_norm = g_out is not None
    side_cast = cast is not None
    any_spec = pl.BlockSpec(memory_space=pl.ANY)
    in_specs = [
        pl.BlockSpec((tm, d), lambda i: (i, 0)),
        pl.BlockSpec((1, d), lambda i: (0, 0)),
        pl.BlockSpec((1, d), lambda i: (0, 0)),
        any_spec, any_spec, any_spec,
    ]
    out_specs = [pl.BlockSpec((tm, d), lambda i: (i, 0))]
    out_shape = [jax.ShapeDtypeStruct((m, d), F32)]
    scratch = [
        pltpu.VMEM((tm, d), BF16),
        pltpu.VMEM((2, d, tf), BF16),
        pltpu.VMEM((2, d, tf), BF16),
        pltpu.VMEM((2, tf, d), BF16),
        pltpu.SemaphoreType.DMA((3, 2)),
    ]
    operands = [x, g.reshape(1, d), (g_out if final_norm else g).reshape(1, d), wg, wu, wd]
    if side_cast:
        assert _can_side_cast(m, d, f) and all(c.shape == w.shape for c, w in zip(cast, (wg, wu, wd)))
        span = d // tiles
        piece_shapes = [(span, tf), (span, tf), (tf, span)]
        in_specs += [any_spec] * 3
        out_specs += [any_spec] * 3
        out_shape += [jax.ShapeDtypeStruct(c.shape, BF16) for c in cast]
        scratch += [pltpu.VMEM((2,) + s, F32) for s in piece_shapes]
        scratch += [pltpu.VMEM((2,) + s, BF16) for s in piece_shapes]
        scratch += [pltpu.SemaphoreType.DMA((3, 2)), pltpu.SemaphoreType.DMA((3, 2))]
        operands += list(cast)
    body = functools.partial(_ffn_body, n_chunks=n_chunks, final_norm=final_norm, side_cast=side_cast)
    out = pl.pallas_call(
        body,
        grid=(tiles,),
        in_specs=in_specs,
        out_specs=out_specs,
        out_shape=out_shape,
        scratch_shapes=scratch,
        compiler_params=_params("arbitrary"),
        name="ffn",
    )(*operands)
    return out if side_cast else out[0]


def _proj_slabs_body(a_ref, w_ref, b_ref, o_ref):
    acc = jnp.dot(a_ref[...], w_ref[...], preferred_element_type=F32) + b_ref[...]
    for c in range(o_ref.shape[0]):
        o_ref[c] = acc[:, c * LANES:(c + 1) * LANES].astype(o_ref.dtype)


def _proj_slabs(a, w, b):
    m, k = a.shape
    n = w.shape[1]
    tm = _tile(m, 1024)
    tn = _tile(n, 1024)
    return pl.pallas_call(
        _proj_slabs_body,
        grid=(m // tm, n // tn),
        in_specs=[
            pl.BlockSpec((tm, k), lambda i, j: (i, 0)),
            pl.BlockSpec((k, tn), lambda i, j: (0, j)),
            pl.BlockSpec((1, tn), lambda i, j: (0, j)),
        ],
        out_specs=pl.BlockSpec((tn // LANES, tm, LANES), lambda i, j: (j, i, 0)),
        out_shape=jax.ShapeDtypeStruct((n // LANES, m, LANES), BF16),
        compiler_params=_params("parallel", "parallel"),
        name="in_proj",
    )(a, w, b.reshape(1, n))


def _matmul_body(a_ref, w_ref, o_ref):
    o_ref[...] = jnp.dot(a_ref[...], w_ref[...], preferred_element_type=F32).astype(o_ref.dtype)


def _matmul(a, w, out_dtype):
    m, k = a.shape
    n = w.shape[1]
    tm = _tile(m, 1024)
    tn = _tile(n, 512)
    return pl.pallas_call(
        _matmul_body,
        grid=(m // tm, n // tn),
        in_specs=[pl.BlockSpec((tm, k), lambda i, j: (i, 0)), pl.BlockSpec((k, tn), lambda i, j: (0, j))],
        out_specs=pl.BlockSpec((tm, tn), lambda i, j: (i, j)),
        out_shape=jax.ShapeDtypeStruct((m, n), out_dtype),
        compiler_params=_params("parallel", "parallel"),
        name="matmul",
    )(a, w)


def _matmul_residual_body(a_ref, w_ref, r_ref, o_ref):
    o_ref[...] = r_ref[...] + jnp.dot(a_ref[...], w_ref[...], preferred_element_type=F32)


def _matmul_residual(a, w, res):
    m, k = a.shape
    n = w.shape[1]
    tm = _tile(m, 1024)
    tn = _tile(n, 512)
    return pl.pallas_call(
        _matmul_residual_body,
        grid=(m // tm, n // tn),
        in_specs=[
            pl.BlockSpec((tm, k), lambda i, j: (i, 0)),
            pl.BlockSpec((k, tn), lambda i, j: (0, j)),
            pl.BlockSpec((tm, tn), lambda i, j: (i, j)),
        ],
        out_specs=pl.BlockSpec((tm, tn), lambda i, j: (i, j)),
        out_shape=jax.ShapeDtypeStruct((m, n), F32),
        compiler_params=_params("parallel", "parallel"),
        name="out_proj",
    )(a, w, res)


def _fold_rows(x, op, rows=8):
    while x.shape[0] > rows:
        half = x.shape[0] // 2
        x = op(x[:half], x[half:])
    return x


def _moba_body(q_ref, k_ref, v_ref, town_ref, tprev_ref, bfar_ref, o_ref, km_ref, sel_ref, s_ref, l_ref, acc_ref,
               *, n_blocks, topk, heads):
    n = pl.program_id(2)
    blk = MOBA_BLOCK
    pair = 2 * blk
    scale = SCORE_SCALE
    group = range(heads)

    @pl.when(n == 0)
    def _():
        for g in group:
            for j in range(n_blocks):
                kj = k_ref[g, 0, j * blk:(j + 1) * blk, :].astype(F32)
                km_ref[g, j:j + 1, :] = jnp.mean(kj, axis=0, keepdims=True)

    q = [q_ref[g, 0] for g in group]

    for g in group:
        gate = lax.dot_general(km_ref[g].astype(BF16), q[g], _NT, preferred_element_type=F32)
        rows = lax.broadcasted_iota(jnp.int32, gate.shape, 0)
        gate = jnp.where(rows < n, gate, -jnp.inf)
        sel = jnp.zeros(gate.shape, jnp.bool_)
        for _ in range(topk):
            best = jnp.max(gate, axis=0, keepdims=True)
            first = jnp.min(jnp.where(gate == best, rows, n_blocks), axis=0, keepdims=True)
            pick = (rows == first) & (best > -jnp.inf)
            sel = sel | pick
            gate = jnp.where(pick, -jnp.inf, gate)
        sel_ref[g] = jnp.where(sel, 0.0, MASKED)

    def scores(g, rows):
        return lax.dot_general(k_ref[g, 0, rows, :], q[g], _NT, preferred_element_type=F32) * scale

    def pv(g, rows, p):
        return lax.dot_general(v_ref[g, 0, rows, :], p.astype(BF16), _TN, preferred_element_type=F32)

    jp = jnp.maximum(n - 1, 0)
    own_rows = pl.ds(pl.multiple_of(n * blk, blk), blk)
    prev_rows = pl.ds(pl.multiple_of(jp * blk, blk), blk)
    m8 = []
    for g in group:
        s_own = scores(g, own_rows) + town_ref[g]
        s_prev = scores(g, prev_rows) + tprev_ref[g] + sel_ref[g, pl.ds(jp, 1), :]
        s_ref[g, 0:blk, :] = s_own
        s_ref[g, blk:pair, :] = s_prev
        m8.append(jnp.maximum(_fold_rows(s_own, jnp.maximum), _fold_rows(s_prev, jnp.maximum)))

    n_far = n - 1
    n_pairs = lax.shift_right_logical(n, 1)

    def far_scores(c, m8):
        key_rows = pl.ds(pl.multiple_of(c * pair, pair), pair)
        base = pl.multiple_of(pair + c * pair, pair)
        j1 = 2 * c + 1
        out = []
        for g in group:
            s = scores(g, key_rows)
            bfar = bfar_ref[g]
            s0 = s[:blk] + (sel_ref[g, pl.ds(2 * c, 1), :] + bfar)
            s1 = s[blk:] + jnp.where(j1 < n_far, sel_ref[g, pl.ds(j1, 1), :] + bfar, MASKED)
            s_ref[g, pl.ds(base, blk), :] = s0
            s_ref[g, pl.ds(base + blk, blk), :] = s1
            out.append(jnp.maximum(m8[g], jnp.maximum(_fold_rows(s0, jnp.maximum), _fold_rows(s1, jnp.maximum))))
        return tuple(out)

    m8 = lax.fori_loop(0, n_pairs, far_scores, tuple(m8))
    m = [jnp.max(m8[g], axis=0, keepdims=True) for g in group]

    for g in group:
        p_own = jnp.exp2(s_ref[g, 0:blk, :] - m[g])
        p_prev = jnp.exp2(s_ref[g, blk:pair, :] - m[g])
        l_ref[g] = _fold_rows(p_own, jnp.add) + _fold_rows(p_prev, jnp.add)
        acc_ref[g] = pv(g, own_rows, p_own) + pv(g, prev_rows, p_prev)

    def far_pv(c, carry):
        key_rows = pl.ds(pl.multiple_of(c * pair, pair), pair)
        base = pl.multiple_of(pair + c * pair, pair)
        for g in group:
            p = jnp.exp2(s_ref[g, pl.ds(base, pair), :] - m[g])
            l_ref[g] += _fold_rows(p, jnp.add)
            acc_ref[g] += pv(g, key_rows, p)
        return carry

    lax.fori_loop(0, n_pairs, far_pv, 0)
    for g in group:
        l = jnp.sum(l_ref[g], axis=0, keepdims=True)
        o_ref[g, 0] = (acc_ref[g] / l).T.astype(o_ref.dtype)


def _moba(proj, town, tprev, bfar):
    _, b, s, _ = proj.shape
    blk = MOBA_BLOCK
    n_blocks = s // blk
    heads = 4
    body = functools.partial(_moba_body, n_blocks=n_blocks, topk=min(MOBA_TOPK, n_blocks), heads=heads)
    return pl.pallas_call(
        body,
        grid=(b, N_MOBA_HEADS // heads, n_blocks),
        in_specs=[
            pl.BlockSpec((heads, 1, blk, HEAD_DIM), lambda bi, h, n: (SLAB_QM // heads + h, bi, n, 0)),
            pl.BlockSpec((heads, 1, s, HEAD_DIM), lambda bi, h, n: (SLAB_KM // heads + h, bi, 0, 0)),
            pl.BlockSpec((heads, 1, s, HEAD_DIM), lambda bi, h, n: (SLAB_VM // heads + h, bi, 0, 0)),
            pl.BlockSpec((heads, blk, blk), lambda bi, h, n: (h, 0, 0)),
            pl.BlockSpec((heads, blk, blk), lambda bi, h, n: (h, 0, 0)),
            pl.BlockSpec((heads, 1, blk), lambda bi, h, n: (h, 0, 0)),
        ],
        out_specs=pl.BlockSpec((heads, 1, blk, HEAD_DIM), lambda bi, h, n: (h, bi, n, 0)),
        out_shape=jax.ShapeDtypeStruct((N_MOBA_HEADS, b, s, HEAD_DIM), F32),
        scratch_shapes=[
            pltpu.VMEM((heads, n_blocks, HEAD_DIM), F32),
            pltpu.VMEM((heads, n_blocks, blk), F32),
            pltpu.VMEM((heads, 2 * blk * (1 + (n_blocks - 1) // 2), blk), F32),
            pltpu.VMEM((heads, 8, blk), F32),
            pltpu.VMEM((heads, HEAD_DIM, blk), F32),
        ],
        compiler_params=_params("parallel", "parallel", "arbitrary"),
        name="moba",
    )(proj, proj, proj, town, tprev, bfar)


def _swa_body(q_ref, k_ref, v_ref, bias_ref, sink_ref, o_ref, *, blocks_per_step):
    blk = SWA_BLOCK
    scale = SCORE_SCALE
    sink = sink_ref[0]
    for r in range(blocks_per_step):
        n = pl.program_id(2) * blocks_per_step + r
        q_rows = slice(r * blk, (r + 1) * blk)
        q = jnp.concatenate([q_ref[g, 0, q_rows, :] for g in range(SWA_GROUP)], axis=0)
        prev_rows = pl.ds(pl.multiple_of(jnp.maximum(n - 1, 0) * blk, blk), blk)
        own_rows = pl.ds(pl.multiple_of(n * blk, blk), blk)
        k = jnp.concatenate([k_ref[0, 0, prev_rows, :], k_ref[0, 0, own_rows, :]], axis=0)
        v = jnp.concatenate([v_ref[0, 0, prev_rows, :], v_ref[0, 0, own_rows, :]], axis=0)
        s = lax.dot_general(k, q, _NT, preferred_element_type=F32) * scale + bias_ref[0]
        key_row = lax.broadcasted_iota(jnp.int32, s.shape, 0)
        s = jnp.where((key_row < blk) & (n == 0), MASKED, s)
        m = jnp.maximum(jnp.max(s, axis=0, keepdims=True), sink)
        p = jnp.exp2(s - m)
        l = jnp.sum(p, axis=0, keepdims=True) + jnp.exp2(sink - m)
        o = lax.dot_general(v, p.astype(BF16), _TN, preferred_element_type=F32) / l
        for g in range(SWA_GROUP):
            o_ref[g, 0, q_rows, :] = o[:, g * blk:(g + 1) * blk].T.astype(o_ref.dtype)


def _swa(proj, bias, sink):
    _, b, s, _ = proj.shape
    blk = SWA_BLOCK
    gq = SWA_GROUP * blk
    per_step = 4 if (s // blk) % 4 == 0 else 1
    rows = per_step * blk
    return pl.pallas_call(
        functools.partial(_swa_body, blocks_per_step=per_step),
        grid=(b, N_SWA_KV_HEADS, s // rows),
        in_specs=[
            pl.BlockSpec((SWA_GROUP, 1, rows, HEAD_DIM), lambda bi, h, n: (SLAB_QS // SWA_GROUP + h, bi, n, 0)),
            pl.BlockSpec((1, 1, s, HEAD_DIM), lambda bi, h, n: (SLAB_KS + h, bi, 0, 0)),
            pl.BlockSpec((1, 1, s, HEAD_DIM), lambda bi, h, n: (SLAB_VS + h, bi, 0, 0)),
            pl.BlockSpec((1, 2 * blk, gq), lambda bi, h, n: (h, 0, 0)),
            pl.BlockSpec((1, 1, gq), lambda bi, h, n: (h, 0, 0)),
        ],
        out_specs=pl.BlockSpec((SWA_GROUP, 1, rows, HEAD_DIM), lambda bi, h, n: (h, bi, n, 0)),
        out_shape=jax.ShapeDtypeStruct((N_SWA_HEADS, b, s, HEAD_DIM), F32),
        compiler_params=_params("parallel", "parallel", "arbitrary"),
        name="swa",
    )(proj, proj, proj, bias, sink)


def _head_norm_body(om_ref, os_ref, gm_ref, gs_ref, o_ref):
    col = 0
    for x_ref, g_ref in ((om_ref, gm_ref), (os_ref, gs_ref)):
        heads = x_ref.shape[0]
        ss = sum(jnp.sum(jnp.square(x_ref[c]), axis=-1, keepdims=True) for c in range(heads))
        r = lax.rsqrt(ss / (heads * HEAD_DIM) + EPS)
        for c in range(heads):
            o_ref[:, col:col + HEAD_DIM] = (x_ref[c] * r * g_ref[:, c * HEAD_DIM:(c + 1) * HEAD_DIM]).astype(o_ref.dtype)
            col += HEAD_DIM


def _head_norm(om, osw, gm, gs):
    hm, m, _ = om.shape
    hs = osw.shape[0]
    tm = _tile(m, 256)
    width = (hm + hs) * HEAD_DIM
    return pl.pallas_call(
        _head_norm_body,
        grid=(m // tm,),
        in_specs=[
            pl.BlockSpec((hm, tm, HEAD_DIM), lambda i: (0, i, 0)),
            pl.BlockSpec((hs, tm, HEAD_DIM), lambda i: (0, i, 0)),
            pl.BlockSpec((1, hm * HEAD_DIM), lambda i: (0, 0)),
            pl.BlockSpec((1, hs * HEAD_DIM), lambda i: (0, 0)),
        ],
        out_specs=pl.BlockSpec((tm, width), lambda i: (i, 0)),
        out_shape=jax.ShapeDtypeStruct((m, width), BF16),
        compiler_params=_params("parallel"),
        name="head_norm",
    )(om, osw, gm.reshape(1, -1), gs.reshape(1, -1))


def _cross_body(x_ref, g_ref, wq_ref, kv_ref, wo_ref, o_ref, h_ref):
    scale = SCORE_SCALE
    tm = x_ref.shape[0]

    def norm_rows(r, carry):
        rows = pl.ds(pl.multiple_of(r * FFN_ROW_CHUNK, FFN_ROW_CHUNK), FFN_ROW_CHUNK)
        h_ref[rows, :] = _rms_rows(x_ref[rows, :], g_ref[...]).astype(BF16)
        return carry

    lax.fori_loop(0, tm // FFN_ROW_CHUNK, norm_rows, 0)
    q = jnp.dot(h_ref[...], wq_ref[...], preferred_element_type=F32).astype(BF16)
    outs = []
    for hd in range(N_CROSS_HEADS):
        cols = slice(hd * HEAD_DIM, (hd + 1) * HEAD_DIM)
        k = kv_ref[0, :, cols]
        v = kv_ref[0, :, CROSS_W + hd * HEAD_DIM:CROSS_W + (hd + 1) * HEAD_DIM]
        s = lax.dot_general(q[:, cols], k, _NT, preferred_element_type=F32) * scale
        p = jnp.exp2(s - jnp.max(s, axis=-1, keepdims=True))
        l = jnp.sum(p, axis=-1, keepdims=True)
        outs.append((jnp.dot(p.astype(BF16), v, preferred_element_type=F32) / l).astype(BF16))
    o = jnp.concatenate(outs, axis=-1)
    o_ref[...] = x_ref[...] + jnp.dot(o, wo_ref[...], preferred_element_type=F32)


def _cross(x, g, wq, kv, wo, seq):
    m, d = x.shape
    n_mem = kv.shape[1]
    tm = _tile(seq, 512)
    assert tm % FFN_ROW_CHUNK == 0
    tiles_per_seq = seq // tm
    resident = pl.Buffered(1)
    return pl.pallas_call(
        _cross_body,
        grid=(m // tm,),
        in_specs=[
            pl.BlockSpec((tm, d), lambda i: (i, 0)),
            pl.BlockSpec((1, d), lambda i: (0, 0)),
            pl.BlockSpec((d, CROSS_W), lambda i: (0, 0), pipeline_mode=resident),
            pl.BlockSpec((1, n_mem, 2 * CROSS_W), lambda i: (i // tiles_per_seq, 0, 0)),
            pl.BlockSpec((CROSS_W, d), lambda i: (0, 0), pipeline_mode=resident),
        ],
        out_specs=pl.BlockSpec((tm, d), lambda i: (i, 0)),
        out_shape=jax.ShapeDtypeStruct((m, d), F32),
        scratch_shapes=[pltpu.VMEM((tm, d), BF16)],
        compiler_params=_params("parallel"),
        name="cross",
    )(x, g.reshape(1, d), wq, kv, wo)


def _t5_bucket(dist):
    n = np.maximum(dist, 0)
    max_exact = NUM_BUCKETS // 2
    nf = np.maximum(n, max_exact).astype(np.float32)
    large = max_exact + (np.log(nf / np.float32(max_exact)) / np.float32(math.log(MAX_DISTANCE / max_exact))
                         * np.float32(NUM_BUCKETS - max_exact)).astype(np.int32)
    large = np.minimum(large, NUM_BUCKETS - 1)
    return np.where(n < max_exact, n, large).astype(np.int32)


def _bias_by_distance(bias_hb, dists):
    onehot = _t5_bucket(dists)[None, :] == np.arange(NUM_BUCKETS)[:, None]
    return jnp.sum(jnp.where(onehot[None], bias_hb[:, :, None], 0.0), axis=1)


def _toeplitz(g, rows, cols):
    heads, n = g.shape
    assert n == rows + cols - 1
    width = rows + cols
    tiled = jnp.broadcast_to(jnp.pad(g, ((0, 0), (0, 1)))[:, None, :], (heads, rows, width))
    skew = tiled.reshape(heads, rows * width)[:, :rows * (width - 1)].reshape(heads, rows, width - 1)
    return skew[:, :, rows - 1:rows - 1 + cols]


def _moba_bias_tables(bias_hb):
    blk = MOBA_BLOCK
    d_own = np.arange(2 * blk - 1) - (blk - 1)
    g_own = jnp.where(d_own >= 0, _bias_by_distance(bias_hb, np.maximum(d_own, 0)), MASKED)
    g_prev = _bias_by_distance(bias_hb, d_own + blk)
    far = int(_t5_bucket(np.array(blk + 1)))
    bfar = jnp.broadcast_to(bias_hb[:, far][:, None, None], (bias_hb.shape[0], 1, blk))
    return _toeplitz(g_own, blk, blk), _toeplitz(g_prev, blk, blk), bfar


def _swa_bias_table(bias_hb):
    blk = SWA_BLOCK
    dist = np.arange(3 * blk - 1) - (blk - 1)
    ok = (dist >= 0) & (dist < SWA_WINDOW)
    g = jnp.where(ok, _bias_by_distance(bias_hb, np.maximum(dist, 0)), MASKED)
    t = _toeplitz(g, 2 * blk, blk)
    t = t.reshape(N_SWA_KV_HEADS, SWA_GROUP, 2 * blk, blk).transpose(0, 2, 1, 3)
    return t.reshape(N_SWA_KV_HEADS, 2 * blk, SWA_GROUP * blk)


def kernel(x, mem, rel_bias, g_final, g_ffn1, w1_gate, w1_up, w1_down, g_mix, w_in, b_in, sinks, g_out_moba, g_out_swa, w_out, g_cross, g_mem, w_cq, w_ck, w_cv, w_co, g_ffn2, w2_gate, w2_up, w2_down):
    b, s, d = x.shape
    n_mem = mem.shape[1]
    depth = w_in.shape[0]
    assert s % MOBA_BLOCK == 0 and s % SWA_BLOCK == 0 and d % LANES == 0

    bias_moba = rel_bias[:, :N_MOBA_HEADS].T.astype(F32)
    bias_swa = rel_bias[:, N_MOBA_HEADS:].T.astype(F32)
    town, tprev, bfar = _moba_bias_tables(bias_moba * LOG2E)
    tswa = _swa_bias_table(bias_swa * LOG2E)

    xt = x.reshape(b * s, d)
    memt = mem.reshape(b * n_mem, d)
    for l in range(depth):
        w1 = [w.astype(BF16) for w in (w1_gate[l], w1_up[l], w1_down[l])]
        w2 = (w2_gate[l], w2_up[l], w2_down[l])
        if _can_side_cast(b * s, d, w1[0].shape[1]) and all(p.shape == q.shape for p, q in zip(w1, w2)):
            xt, *w2 = _ffn(xt, g_ffn1[l], *w1, cast=w2)
        else:
            xt = _ffn(xt, g_ffn1[l], *w1)
            w2 = [w.astype(BF16) for w in w2]

        nrm = _rmsnorm(xt, g_mix[l], BF16)
        proj = _proj_slabs(nrm, w_in[l].astype(BF16), b_in[l]).reshape(N_SLABS, b, s, HEAD_DIM)
        o_m = _moba(proj, town, tprev, bfar).reshape(N_MOBA_HEADS, b * s, HEAD_DIM)
        sink = jnp.broadcast_to((sinks[l].astype(F32) * LOG2E).reshape(N_SWA_KV_HEADS, SWA_GROUP, 1),
                                (N_SWA_KV_HEADS, SWA_GROUP, SWA_BLOCK)).reshape(N_SWA_KV_HEADS, 1, SWA_GROUP * SWA_BLOCK)
        o_s = _swa(proj, tswa, sink).reshape(N_SWA_HEADS, b * s, HEAD_DIM)
        o = _head_norm(o_m, o_s, g_out_moba[l], g_out_swa[l])
        xt = _matmul_residual(o, w_out[l].astype(BF16), xt)

        mem_n = _rmsnorm(memt, g_mem[l], BF16)
        w_kv = jnp.concatenate([w_ck[l], w_cv[l]], axis=1).astype(BF16)
        kv = _matmul(mem_n, w_kv, BF16).reshape(b, n_mem, 2 * CROSS_W)
        xt = _cross(xt, g_cross[l], w_cq[l].astype(BF16), kv, w_co[l].astype(BF16), s)

        xt = _ffn(xt, g_ffn2[l], *w2, g_out=g_final if l == depth - 1 else None)
    return xt.reshape(b, s, d)
```

```python
import functools
import math

import numpy as np
import jax
import jax.numpy as jnp
from jax import lax
from jax.experimental import pallas as pl
from jax.experimental.pallas import tpu as pltpu

HEAD_DIM = 128
N_MOBA_HEADS = 16
N_SWA_HEADS = 16
N_SWA_KV_HEADS = 4
SWA_GROUP = N_SWA_HEADS // N_SWA_KV_HEADS
MOBA_BLOCK = 256
MOBA_TOPK = 3
SWA_WINDOW = 128
SWA_BLOCK = 128
N_CROSS_HEADS = 4
NUM_BUCKETS = 32
MAX_DISTANCE = 128
EPS = 1e-6

MOBA_W = N_MOBA_HEADS * HEAD_DIM
SWA_W = N_SWA_HEADS * HEAD_DIM
SWA_KV_W = N_SWA_KV_HEADS * HEAD_DIM
CROSS_W = N_CROSS_HEADS * HEAD_DIM

SLAB_QM = 0
SLAB_KM = SLAB_QM + N_MOBA_HEADS
SLAB_VM = SLAB_KM + N_MOBA_HEADS
SLAB_QS = SLAB_VM + N_MOBA_HEADS
SLAB_KS = SLAB_QS + N_SWA_HEADS
SLAB_VS = SLAB_KS + N_SWA_KV_HEADS
N_SLABS = SLAB_VS + N_SWA_KV_HEADS

MASKED = -1e30
LOG2E = math.log2(math.e)
SCORE_SCALE = HEAD_DIM ** -0.5 * LOG2E
LANES = 128
V7X_VMEM_LIMIT_BYTES = 56 * 1024 * 1024

F32 = jnp.float32
BF16 = jnp.bfloat16

_NT = (((1,), (1,)), ((), ()))
_TN = (((0,), (0,)), ((), ()))


def _tile(n, want):
    if n <= want:
        return n
    t = want
    while n % t:
        t -= 8
    return t


def _params(*semantics):
    return pltpu.CompilerParams(dimension_semantics=semantics, vmem_limit_bytes=V7X_VMEM_LIMIT_BYTES)


def _rms_rows(x, g):
    ms = jnp.mean(x * x, axis=-1, keepdims=True)
    return x * lax.rsqrt(ms + EPS) * g


def _rmsnorm_body(x_ref, g_ref, o_ref):
    o_ref[...] = _rms_rows(x_ref[...], g_ref[...]).astype(o_ref.dtype)


def _rmsnorm(x, g, out_dtype):
    m, d = x.shape
    tm = _tile(m, 256)
    return pl.pallas_call(
        _rmsnorm_body,
        grid=(m // tm,),
        in_specs=[pl.BlockSpec((tm, d), lambda i: (i, 0)), pl.BlockSpec((1, d), lambda i: (0, 0))],
        out_specs=pl.BlockSpec((tm, d), lambda i: (i, 0)),
        out_shape=jax.ShapeDtypeStruct((m, d), out_dtype),
        compiler_params=_params("parallel"),
        name="rmsnorm",
    )(x, g.reshape(1, d))


FFN_ROW_CHUNK = 64


def _ffn_body(*refs, n_chunks, final_norm, side_cast):
    x_ref, g_ref, gout_ref, wg_hbm, wu_hbm, wd_hbm = refs[:6]
    refs = refs[6:]
    if side_cast:
        src_hbm, refs = refs[:3], refs[3:]
    o_ref, refs = refs[0], refs[1:]
    if side_cast:
        dst_hbm, refs = refs[:3], refs[3:]
    h_ref, wg_buf, wu_buf, wd_buf, sem = refs[:5]
    if side_cast:
        cin, cout, cin_sem, cout_sem = refs[5:8], refs[8:11], refs[11], refs[12]
    i = pl.program_id(0)
    tm = x_ref.shape[0]
    tf = wd_buf.shape[1]
    last = n_chunks - 1
    backwards = lax.rem(i, 2) == 1

    def fetch(pos, slot):
        cols = pl.ds(pl.multiple_of(jnp.where(backwards, last - pos, pos) * tf, tf), tf)
        return (pltpu.make_async_copy(wg_hbm.at[:, cols], wg_buf.at[slot], sem.at[0, slot]),
                pltpu.make_async_copy(wu_hbm.at[:, cols], wu_buf.at[slot], sem.at[1, slot]),
                pltpu.make_async_copy(wd_hbm.at[cols, :], wd_buf.at[slot], sem.at[2, slot]))

    def start(pos, slot):
        for copy in fetch(pos, slot):
            copy.start()

    def wait(pos, slot):
        for copy in fetch(pos, slot):
            copy.wait()

    def pieces(hbm, pos):
        span = cin[0].shape[1]
        own = pl.ds(pl.multiple_of(i * span, span), span)
        chunk = pl.ds(pl.multiple_of(pos * tf, tf), tf)
        return hbm[0].at[own, chunk], hbm[1].at[own, chunk], hbm[2].at[chunk, own]

    def cast_in(pos, slot):
        return [pltpu.make_async_copy(src, cin[k].at[slot], cin_sem.at[k, slot])
                for k, src in enumerate(pieces(src_hbm, pos))]

    def cast_out(pos, slot):
        return [pltpu.make_async_copy(cout[k].at[slot], dst, cout_sem.at[k, slot])
                for k, dst in enumerate(pieces(dst_hbm, pos))]

    def side(copies, method, when=None):
        if not side_cast:
            return

        def go():
            for copy in copies():
                getattr(copy, method)()

        if when is None:
            go()
        else:
            pl.when(when)(go)

    @pl.when(i == 0)
    def _():
        start(0, 0)

    side(lambda: cast_in(0, 0), "start")

    def norm_rows(r, carry):
        rows = pl.ds(pl.multiple_of(r * FFN_ROW_CHUNK, FFN_ROW_CHUNK), FFN_ROW_CHUNK)
        x = x_ref[rows, :]
        o_ref[rows, :] = x
        inv = lax.rsqrt(jnp.mean(x * x, axis=-1, keepdims=True) + EPS)
        h_ref[rows, :] = (o_ref[rows, :] * inv * g_ref[...]).astype(BF16)
        return carry

    lax.fori_loop(0, tm // FFN_ROW_CHUNK, norm_rows, 0)

    @pl.when(i == 0)
    def _():
        wait(0, 0)

    def compute(slot):
        if side_cast:
            for k in range(3):
                cout[k][slot] = cin[k][slot].astype(BF16)
        h = h_ref[...]
        gate = jnp.dot(h, wg_buf[slot], preferred_element_type=F32)
        up = jnp.dot(h, wu_buf[slot], preferred_element_type=F32)
        act = (0.5 * gate * jax.nn.sigmoid(gate) * up).astype(BF16)
        o_ref[...] += jnp.dot(act, wd_buf[slot], preferred_element_type=F32)

    def chunk_pair(pp, carry):
        pos = 2 * pp
        side(lambda: cast_in(pos, 0), "wait")
        side(lambda: cast_out(pos - 1, 1), "start", pos >= 1)
        side(lambda: cast_out(pos - 2, 0), "wait", pos >= 2)

        @pl.when(pos < last)
        def _():
            start(pos + 1, 1)
            side(lambda: cast_in(pos + 1, 1), "start")

        compute(0)

        @pl.when(pos < last)
        def _():
            wait(pos + 1, 1)
            side(lambda: cast_in(pos + 1, 1), "wait")
            side(lambda: cast_out(pos, 0), "start")
            side(lambda: cast_out(pos - 1, 1), "wait", pos >= 1)
            start(pos + 2, 0)
            side(lambda: cast_in(pos + 2, 0), "start")
            compute(1)
            wait(pos + 2, 0)

        return carry

    lax.fori_loop(0, (n_chunks + 1) // 2, chunk_pair, 0)
    side(lambda: cast_out(last, 0), "start")
    side(lambda: cast_out(last - 1, 1), "wait")
    side(lambda: cast_out(last, 0), "wait")

    if final_norm:
        def out_rows(r, carry):
            rows = pl.ds(pl.multiple_of(r * FFN_ROW_CHUNK, FFN_ROW_CHUNK), FFN_ROW_CHUNK)
            o_ref[rows, :] = _rms_rows(o_ref[rows, :], gout_ref[...])
            return carry

        lax.fori_loop(0, tm // FFN_ROW_CHUNK, out_rows, 0)


def _ffn_tiles(m, d, f):
    tm = _tile(m, 512)
    tf = 256 if f % 256 == 0 else LANES
    return tm, tf


def _can_side_cast(m, d, f):
    tm, tf = _ffn_tiles(m, d, f)
    tiles = m // tm
    return d % tiles == 0 and (d // tiles) % LANES == 0


def _ffn(x, g, wg, wu, wd, g_out=None, cast=None):
    m, d = x.shape
    f = wg.shape[1]
    tm, tf = _ffn_tiles(m, d, f)
    n_chunks = f // tf
    tiles = m // tm
    assert f % tf == 0 and n_chunks % 2 == 1 and tm % FFN_ROW_CHUNK == 0
    final_norm = g_out is not None
    side_cast = cast is not None
    any_spec = pl.BlockSpec(memory_space=pl.ANY)
    in_specs = [
        pl.BlockSpec((tm, d), lambda i: (i, 0)),
        pl.BlockSpec((1, d), lambda i: (0, 0)),
        pl.BlockSpec((1, d), lambda i: (0, 0)),
        any_spec, any_spec, any_spec,
    ]
    out_specs = [pl.BlockSpec((tm, d), lambda i: (i, 0))]
    out_shape = [jax.ShapeDtypeStruct((m, d), F32)]
    scratch = [
        pltpu.VMEM((tm, d), BF16),
        pltpu.VMEM((2, d, tf), BF16),
        pltpu.VMEM((2, d, tf), BF16),
        pltpu.VMEM((2, tf, d), BF16),
        pltpu.SemaphoreType.DMA((3, 2)),
    ]
    operands = [x, g.reshape(1, d), (g_out if final_norm else g).reshape(1, d), wg, wu, wd]
    if side_cast:
        assert _can_side_cast(m, d, f) and all(c.shape == w.shape for c, w in zip(cast, (wg, wu, wd)))
        span = d // tiles
        piece_shapes = [(span, tf), (span, tf), (tf, span)]
        in_specs += [any_spec] * 3
        out_specs += [any_spec] * 3
        out_shape += [jax.ShapeDtypeStruct(c.shape, BF16) for c in cast]
        scratch += [pltpu.VMEM((2,) + s, F32) for s in piece_shapes]
        scratch += [pltpu.VMEM((2,) + s, BF16) for s in piece_shapes]
        scratch += [pltpu.SemaphoreType.DMA((3, 2)), pltpu.SemaphoreType.DMA((3, 2))]
        operands += list(cast)
    body = functools.partial(_ffn_body, n_chunks=n_chunks, final_norm=final_norm, side_cast=side_cast)
    out = pl.pallas_call(
        body,
        grid=(tiles,),
        in_specs=in_specs,
        out_specs=out_specs,
        out_shape=out_shape,
        scratch_shapes=scratch,
        compiler_params=_params("arbitrary"),
        name="ffn",
    )(*operands)
    return out if side_cast else out[0]


def _proj_slabs_body(a_ref, w_ref, b_ref, o_ref):
    acc = jnp.dot(a_ref[...], w_ref[...], preferred_element_type=F32) + b_ref[...]
    for c in range(o_ref.shape[0]):
        o_ref[c] = acc[:, c * LANES:(c + 1) * LANES].astype(o_ref.dtype)


def _proj_slabs(a, w, b):
    m, k = a.shape
    n = w.shape[1]
    tm = _tile(m, 1024)
    tn = _tile(n, 1024)
    return pl.pallas_call(
        _proj_slabs_body,
        grid=(m // tm, n // tn),
        in_specs=[
            pl.BlockSpec((tm, k), lambda i, j: (i, 0)),
            pl.BlockSpec((k, tn), lambda i, j: (0, j)),
            pl.BlockSpec((1, tn), lambda i, j: (0, j)),
        ],
        out_specs=pl.BlockSpec((tn // LANES, tm, LANES), lambda i, j: (j, i, 0)),
        out_shape=jax.ShapeDtypeStruct((n // LANES, m, LANES), BF16),
        compiler_params=_params("parallel", "parallel"),
        name="in_proj",
    )(a, w, b.reshape(1, n))


def _matmul_body(a_ref, w_ref, o_ref):
    o_ref[...] = jnp.dot(a_ref[...], w_ref[...], preferred_element_type=F32).astype(o_ref.dtype)


def _matmul(a, w, out_dtype):
    m, k = a.shape
    n = w.shape[1]
    tm = _tile(m, 1024)
    tn = _tile(n, 512)
    return pl.pallas_call(
        _matmul_body,
        grid=(m // tm, n // tn),
        in_specs=[pl.BlockSpec((tm, k), lambda i, j: (i, 0)), pl.BlockSpec((k, tn), lambda i, j: (0, j))],
        out_specs=pl.BlockSpec((tm, tn), lambda i, j: (i, j)),
        out_shape=jax.ShapeDtypeStruct((m, n), out_dtype),
        compiler_params=_params("parallel", "parallel"),
        name="matmul",
    )(a, w)


def _matmul_residual_body(a_ref, w_ref, r_ref, o_ref):
    o_ref[...] = r_ref[...] + jnp.dot(a_ref[...], w_ref[...], preferred_element_type=F32)


def _matmul_residual(a, w, res):
    m, k = a.shape
    n = w.shape[1]
    tm = _tile(m, 1024)
    tn = _tile(n, 512)
    return pl.pallas_call(
        _matmul_residual_body,
        grid=(m // tm, n // tn),
        in_specs=[
            pl.BlockSpec((tm, k), lambda i, j: (i, 0)),
            pl.BlockSpec((k, tn), lambda i, j: (0, j)),
            pl.BlockSpec((tm, tn), lambda i, j: (i, j)),
        ],
        out_specs=pl.BlockSpec((tm, tn), lambda i, j: (i, j)),
        out_shape=jax.ShapeDtypeStruct((m, n), F32),
        compiler_params=_params("parallel", "parallel"),
        name="out_proj",
    )(a, w, res)


def _fold_rows(x, op, rows=8):
    while x.shape[0] > rows:
        half = x.shape[0] // 2
        x = op(x[:half], x[half:])
    return x


def _moba_body(q_ref, k_ref, v_ref, town_ref, tprev_ref, bfar_ref, o_ref, km_ref, vt_ref, sel_ref, s_ref, l_ref, acc_ref,
               *, n_blocks, topk, heads):
    n = pl.program_id(2)
    blk = MOBA_BLOCK
    pair = 2 * blk
    scale = SCORE_SCALE
    group = range(heads)

    @pl.when(n == 0)
    def _():
        for g in group:
            for j in range(n_blocks):
                kj = k_ref[g, 0, j * blk:(j + 1) * blk, :].astype(F32)
                km_ref[g, j:j + 1, :] = jnp.mean(kj, axis=0, keepdims=True)
                vt_ref[g, :, j * blk:(j + 1) * blk] = v_ref[g, 0, j * blk:(j + 1) * blk, :].T

    qt = [q_ref[g, 0].T for g in group]

    for g in group:
        gate = jnp.dot(km_ref[g].astype(BF16), qt[g], preferred_element_type=F32)
        rows = lax.broadcasted_iota(jnp.int32, gate.shape, 0)
        gate = jnp.where(rows < n, gate, -jnp.inf)
        sel = jnp.zeros(gate.shape, jnp.bool_)
        for _ in range(topk):
            best = jnp.max(gate, axis=0, keepdims=True)
            first = jnp.min(jnp.where(gate == best, rows, n_blocks), axis=0, keepdims=True)
            pick = (rows == first) & (best > -jnp.inf)
            sel = sel | pick
            gate = jnp.where(pick, -jnp.inf, gate)
        sel_ref[g] = jnp.where(sel, 0.0, MASKED)

    def scores(g, rows):
        return jnp.dot(k_ref[g, 0, rows, :], qt[g], preferred_element_type=F32) * scale

    def pv(g, rows, p):
        return jnp.dot(vt_ref[g, :, rows], p.astype(BF16), preferred_element_type=F32)

    jp = jnp.maximum(n - 1, 0)
    own_rows = pl.ds(pl.multiple_of(n * blk, blk), blk)
    prev_rows = pl.ds(pl.multiple_of(jp * blk, blk), blk)
    m8 = []
    for g in group:
        s_own = scores(g, own_rows) + town_ref[g]
        s_prev = scores(g, prev_rows) + tprev_ref[g] + sel_ref[g, pl.ds(jp, 1), :]
        s_ref[g, 0:blk, :] = s_own
        s_ref[g, blk:pair, :] = s_prev
        m8.append(jnp.maximum(_fold_rows(s_own, jnp.maximum), _fold_rows(s_prev, jnp.maximum)))

    n_far = n - 1
    n_pairs = lax.shift_right_logical(n, 1)

    def far_scores(c, m8):
        key_rows = pl.ds(pl.multiple_of(c * pair, pair), pair)
        base = pl.multiple_of(pair + c * pair, pair)
        j1 = 2 * c + 1
        out = []
        for g in group:
            s = scores(g, key_rows)
            bfar = bfar_ref[g]
            s0 = s[:blk] + (sel_ref[g, pl.ds(2 * c, 1), :] + bfar)
            s1 = s[blk:] + jnp.where(j1 < n_far, sel_ref[g, pl.ds(j1, 1), :] + bfar, MASKED)
            s_ref[g, pl.ds(base, blk), :] = s0
            s_ref[g, pl.ds(base + blk, blk), :] = s1
            out.append(jnp.maximum(m8[g], jnp.maximum(_fold_rows(s0, jnp.maximum), _fold_rows(s1, jnp.maximum))))
        return tuple(out)

    m8 = lax.fori_loop(0, n_pairs, far_scores, tuple(m8))
    m = [jnp.max(m8[g], axis=0, keepdims=True) for g in group]

    for g in group:
        p_own = jnp.exp2(s_ref[g, 0:blk, :] - m[g])
        p_prev = jnp.exp2(s_ref[g, blk:pair, :] - m[g])
        l_ref[g] = _fold_rows(p_own, jnp.add) + _fold_rows(p_prev, jnp.add)
        acc_ref[g] = pv(g, own_rows, p_own) + pv(g, prev_rows, p_prev)

    def far_pv(c, carry):
        key_rows = pl.ds(pl.multiple_of(c * pair, pair), pair)
        base = pl.multiple_of(pair + c * pair, pair)
        for g in group:
            p = jnp.exp2(s_ref[g, pl.ds(base, pair), :] - m[g])
            l_ref[g] += _fold_rows(p, jnp.add)
            acc_ref[g] += pv(g, key_rows, p)
        return carry

    lax.fori_loop(0, n_pairs, far_pv, 0)
    for g in group:
        l = jnp.sum(l_ref[g], axis=0, keepdims=True)
        o_ref[g, 0] = (acc_ref[g] / l).T.astype(o_ref.dtype)


def _moba(proj, town, tprev, bfar):
    _, b, s, _ = proj.shape
    blk = MOBA_BLOCK
    n_blocks = s // blk
    heads = 4
    body = functools.partial(_moba_body, n_blocks=n_blocks, topk=min(MOBA_TOPK, n_blocks), heads=heads)
    return pl.pallas_call(
        body,
        grid=(b, N_MOBA_HEADS // heads, n_blocks),
        in_specs=[
            pl.BlockSpec((heads, 1, blk, HEAD_DIM), lambda bi, h, n: (SLAB_QM // heads + h, bi, n, 0)),
            pl.BlockSpec((heads, 1, s, HEAD_DIM), lambda bi, h, n: (SLAB_KM // heads + h, bi, 0, 0)),
            pl.BlockSpec((heads, 1, s, HEAD_DIM), lambda bi, h, n: (SLAB_VM // heads + h, bi, 0, 0)),
            pl.BlockSpec((heads, blk, blk), lambda bi, h, n: (h, 0, 0)),
            pl.BlockSpec((heads, blk, blk), lambda bi, h, n: (h, 0, 0)),
            pl.BlockSpec((heads, 1, blk), lambda bi, h, n: (h, 0, 0)),
        ],
        out_specs=pl.BlockSpec((heads, 1, blk, HEAD_DIM), lambda bi, h, n: (h, bi, n, 0)),
        out_shape=jax.ShapeDtypeStruct((N_MOBA_HEADS, b, s, HEAD_DIM), F32),
        scratch_shapes=[
            pltpu.VMEM((heads, n_blocks, HEAD_DIM), F32),
            pltpu.VMEM((heads, HEAD_DIM, s), BF16),
            pltpu.VMEM((heads, n_blocks, blk), F32),
            pltpu.VMEM((heads, 2 * blk * (1 + (n_blocks - 1) // 2), blk), F32),
            pltpu.VMEM((heads, 8, blk), F32),
            pltpu.VMEM((heads, HEAD_DIM, blk), F32),
        ],
        compiler_params=_params("parallel", "parallel", "arbitrary"),
        name="moba",
    )(proj, proj, proj, town, tprev, bfar)


def _swa_body(q_ref, k_ref, v_ref, bias_ref, sink_ref, o_ref, *, blocks_per_step):
    blk = SWA_BLOCK
    scale = SCORE_SCALE
    sink = sink_ref[0]
    blocks = range(blocks_per_step)

    def band(ref, r):
        n = pl.program_id(2) * blocks_per_step + r
        prev_rows = pl.ds(pl.multiple_of(jnp.maximum(n - 1, 0) * blk, blk), blk)
        own_rows = pl.ds(pl.multiple_of(n * blk, blk), blk)
        return jnp.concatenate([ref[0, 0, prev_rows, :], ref[0, 0, own_rows, :]], axis=0)

    raw = []
    for r in blocks:
        q = jnp.concatenate([q_ref[g, 0, r * blk:(r + 1) * blk, :] for g in range(SWA_GROUP)], axis=0)
        raw.append(lax.dot_general(band(k_ref, r), q, _NT, preferred_element_type=F32))
    soft = []
    for r in blocks:
        n = pl.program_id(2) * blocks_per_step + r
        s = raw[r] * scale + bias_ref[0]
        key_row = lax.broadcasted_iota(jnp.int32, s.shape, 0)
        s = jnp.where((key_row < blk) & (n == 0), MASKED, s)
        m = jnp.maximum(jnp.max(s, axis=0, keepdims=True), sink)
        p = jnp.exp2(s - m)
        soft.append((p.astype(BF16), jnp.sum(p, axis=0, keepdims=True) + jnp.exp2(sink - m)))
    for r in blocks:
        p, l = soft[r]
        o = lax.dot_general(band(v_ref, r), p, _TN, preferred_element_type=F32) / l
        for g in range(SWA_GROUP):
            o_ref[g, 0, r * blk:(r + 1) * blk, :] = o[:, g * blk:(g + 1) * blk].T.astype(o_ref.dtype)


def _swa(proj, bias, sink):
    _, b, s, _ = proj.shape
    blk = SWA_BLOCK
    gq = SWA_GROUP * blk
    per_step = 8 if (s // blk) % 8 == 0 else 1
    rows = per_step * blk
    return pl.pallas_call(
        functools.partial(_swa_body, blocks_per_step=per_step),
        grid=(b, N_SWA_KV_HEADS, s // rows),
        in_specs=[
            pl.BlockSpec((SWA_GROUP, 1, rows, HEAD_DIM), lambda bi, h, n: (SLAB_QS // SWA_GROUP + h, bi, n, 0)),
            pl.BlockSpec((1, 1, s, HEAD_DIM), lambda bi, h, n: (SLAB_KS + h, bi, 0, 0)),
            pl.BlockSpec((1, 1, s, HEAD_DIM), lambda bi, h, n: (SLAB_VS + h, bi, 0, 0)),
            pl.BlockSpec((1, 2 * blk, gq), lambda bi, h, n: (h, 0, 0)),
            pl.BlockSpec((1, 1, gq), lambda bi, h, n: (h, 0, 0)),
        ],
        out_specs=pl.BlockSpec((SWA_GROUP, 1, rows, HEAD_DIM), lambda bi, h, n: (h, bi, n, 0)),
        out_shape=jax.ShapeDtypeStruct((N_SWA_HEADS, b, s, HEAD_DIM), F32),
        compiler_params=_params("parallel", "parallel", "arbitrary"),
        name="swa",
    )(proj, proj, proj, bias, sink)


def _head_norm_body(om_ref, os_ref, gm_ref, gs_ref, o_ref):
    col = 0
    for x_ref, g_ref in ((om_ref, gm_ref), (os_ref, gs_ref)):
        heads = x_ref.shape[0]
        ss = sum(jnp.sum(jnp.square(x_ref[c]), axis=-1, keepdims=True) for c in range(heads))
        r = lax.rsqrt(ss / (heads * HEAD_DIM) + EPS)
        for c in range(heads):
            o_ref[:, col:col + HEAD_DIM] = (x_ref[c] * r * g_ref[:, c * HEAD_DIM:(c + 1) * HEAD_DIM]).astype(o_ref.dtype)
            col += HEAD_DIM


def _head_norm(om, osw, gm, gs):
    hm, m, _ = om.shape
    hs = osw.shape[0]
    tm = _tile(m, 256)
    width = (hm + hs) * HEAD_DIM
    return pl.pallas_call(
        _head_norm_body,
        grid=(m // tm,),
        in_specs=[
            pl.BlockSpec((hm, tm, HEAD_DIM), lambda i: (0, i, 0)),
            pl.BlockSpec((hs, tm, HEAD_DIM), lambda i: (0, i, 0)),
            pl.BlockSpec((1, hm * HEAD_DIM), lambda i: (0, 0)),
            pl.BlockSpec((1, hs * HEAD_DIM), lambda i: (0, 0)),
        ],
        out_specs=pl.BlockSpec((tm, width), lambda i: (i, 0)),
        out_shape=jax.ShapeDtypeStruct((m, width), BF16),
        compiler_params=_params("parallel"),
        name="head_norm",
    )(om, osw, gm.reshape(1, -1), gs.reshape(1, -1))


def _cross_body(x_ref, g_ref, wq_ref, kv_ref, wo_ref, o_ref, h_ref):
    scale = SCORE_SCALE
    tm = x_ref.shape[0]

    def norm_rows(r, carry):
        rows = pl.ds(pl.multiple_of(r * FFN_ROW_CHUNK, FFN_ROW_CHUNK), FFN_ROW_CHUNK)
        h_ref[rows, :] = _rms_rows(x_ref[rows, :], g_ref[...]).astype(BF16)
        return carry

    lax.fori_loop(0, tm // FFN_ROW_CHUNK, norm_rows, 0)
    q = jnp.dot(h_ref[...], wq_ref[...], preferred_element_type=F32).astype(BF16)
    outs = []
    for hd in range(N_CROSS_HEADS):
        cols = slice(hd * HEAD_DIM, (hd + 1) * HEAD_DIM)
        k = kv_ref[0, :, cols]
        v = kv_ref[0, :, CROSS_W + hd * HEAD_DIM:CROSS_W + (hd + 1) * HEAD_DIM]
        s = lax.dot_general(q[:, cols], k, _NT, preferred_element_type=F32) * scale
        p = jnp.exp2(s - jnp.max(s, axis=-1, keepdims=True))
        l = jnp.sum(p, axis=-1, keepdims=True)
        outs.append((jnp.dot(p.astype(BF16), v, preferred_element_type=F32) / l).astype(BF16))
    o = jnp.concatenate(outs, axis=-1)
    o_ref[...] = x_ref[...] + jnp.dot(o, wo_ref[...], preferred_element_type=F32)


def _cross(x, g, wq, kv, wo, seq):
    m, d = x.shape
    n_mem = kv.shape[1]
    tm = _tile(seq, 512)
    assert tm % FFN_ROW_CHUNK == 0
    tiles_per_seq = seq // tm
    resident = pl.Buffered(1)
    return pl.pallas_call(
        _cross_body,
        grid=(m // tm,),
        in_specs=[
            pl.BlockSpec((tm, d), lambda i: (i, 0)),
            pl.BlockSpec((1, d), lambda i: (0, 0)),
            pl.BlockSpec((d, CROSS_W), lambda i: (0, 0), pipeline_mode=resident),
            pl.BlockSpec((1, n_mem, 2 * CROSS_W), lambda i: (i // tiles_per_seq, 0, 0)),
            pl.BlockSpec((CROSS_W, d), lambda i: (0, 0), pipeline_mode=resident),
        ],
        out_specs=pl.BlockSpec((tm, d), lambda i: (i, 0)),
        out_shape=jax.ShapeDtypeStruct((m, d), F32),
        scratch_shapes=[pltpu.VMEM((tm, d), BF16)],
        compiler_params=_params("parallel"),
        name="cross",
    )(x, g.reshape(1, d), wq, kv, wo)


def _t5_bucket(dist):
    n = np.maximum(dist, 0)
    max_exact = NUM_BUCKETS // 2
    nf = np.maximum(n, max_exact).astype(np.float32)
    large = max_exact + (np.log(nf / np.float32(max_exact)) / np.float32(math.log(MAX_DISTANCE / max_exact))
                         * np.float32(NUM_BUCKETS - max_exact)).astype(np.int32)
    large = np.minimum(large, NUM_BUCKETS - 1)
    return np.where(n < max_exact, n, large).astype(np.int32)


def _bias_by_distance(bias_hb, dists):
    onehot = _t5_bucket(dists)[None, :] == np.arange(NUM_BUCKETS)[:, None]
    return jnp.sum(jnp.where(onehot[None], bias_hb[:, :, None], 0.0), axis=1)


def _toeplitz(g, rows, cols):
    heads, n = g.shape
    assert n == rows + cols - 1
    width = rows + cols
    tiled = jnp.broadcast_to(jnp.pad(g, ((0, 0), (0, 1)))[:, None, :], (heads, rows, width))
    skew = tiled.reshape(heads, rows * width)[:, :rows * (width - 1)].reshape(heads, rows, width - 1)
    return skew[:, :, rows - 1:rows - 1 + cols]


def _moba_bias_tables(bias_hb):
    blk = MOBA_BLOCK
    d_own = np.arange(2 * blk - 1) - (blk - 1)
    g_own = jnp.where(d_own >= 0, _bias_by_distance(bias_hb, np.maximum(d_own, 0)), MASKED)
    g_prev = _bias_by_distance(bias_hb, d_own + blk)
    far = int(_t5_bucket(np.array(blk + 1)))
    bfar = jnp.broadcast_to(bias_hb[:, far][:, None, None], (bias_hb.shape[0], 1, blk))
    return _toeplitz(g_own, blk, blk), _toeplitz(g_prev, blk, blk), bfar


def _swa_bias_table(bias_hb):
    blk = SWA_BLOCK
    dist = np.arange(3 * blk - 1) - (blk - 1)
    ok = (dist >= 0) & (dist < SWA_WINDOW)
    g = jnp.where(ok, _bias_by_distance(bias_hb, np.maximum(dist, 0)), MASKED)
    t = _toeplitz(g, 2 * blk, blk)
    t = t.reshape(N_SWA_KV_HEADS, SWA_GROUP, 2 * blk, blk).transpose(0, 2, 1, 3)
    return t.reshape(N_SWA_KV_HEADS, 2 * blk, SWA_GROUP * blk)


def kernel(x, mem, rel_bias, g_final, g_ffn1, w1_gate, w1_up, w1_down, g_mix, w_in, b_in, sinks, g_out_moba, g_out_swa, w_out, g_cross, g_mem, w_cq, w_ck, w_cv, w_co, g_ffn2, w2_gate, w2_up, w2_down):
    b, s, d = x.shape
    n_mem = mem.shape[1]
    depth = w_in.shape[0]
    assert s % MOBA_BLOCK == 0 and s % SWA_BLOCK == 0 and d % LANES == 0

    bias_moba = rel_bias[:, :N_MOBA_HEADS].T.astype(F32)
    bias_swa = rel_bias[:, N_MOBA_HEADS:].T.astype(F32)
    town, tprev, bfar = _moba_bias_tables(bias_moba * LOG2E)
    tswa = _swa_bias_table(bias_swa * LOG2E)

    xt = x.reshape(b * s, d)
    memt = mem.reshape(b * n_mem, d)
    for l in range(depth):
        w1 = [w.astype(BF16) for w in (w1_gate[l], w1_up[l], w1_down[l])]
        w2 = (w2_gate[l], w2_up[l], w2_down[l])
        if _can_side_cast(b * s, d, w1[0].shape[1]) and all(p.shape == q.shape for p, q in zip(w1, w2)):
            xt, *w2 = _ffn(xt, g_ffn1[l], *w1, cast=w2)
        else:
            xt = _ffn(xt, g_ffn1[l], *w1)
            w2 = [w.astype(BF16) for w in w2]

        nrm = _rmsnorm(xt, g_mix[l], BF16)
        proj = _proj_slabs(nrm, w_in[l].astype(BF16), b_in[l]).reshape(N_SLABS, b, s, HEAD_DIM)
        o_m = _moba(proj, town, tprev, bfar).reshape(N_MOBA_HEADS, b * s, HEAD_DIM)
        sink = jnp.broadcast_to((sinks[l].astype(F32) * LOG2E).reshape(N_SWA_KV_HEADS, SWA_GROUP, 1),
                                (N_SWA_KV_HEADS, SWA_GROUP, SWA_BLOCK)).reshape(N_SWA_KV_HEADS, 1, SWA_GROUP * SWA_BLOCK)
        o_s = _swa(proj, tswa, sink).reshape(N_SWA_HEADS, b * s, HEAD_DIM)
        o = _head_norm(o_m, o_s, g_out_moba[l], g_out_swa[l])
        xt = _matmul_residual(o, w_out[l].astype(BF16), xt)

        mem_n = _rmsnorm(memt, g_mem[l], BF16)
        w_kv = jnp.concatenate([w_ck[l], w_cv[l]], axis=1).astype(BF16)
        kv = _matmul(mem_n, w_kv, BF16).reshape(b, n_mem, 2 * CROSS_W)
        xt = _cross(xt, g_cross[l], w_cq[l].astype(BF16), kv, w_co[l].astype(BF16), s)

        xt = _ffn(xt, g_ffn2[l], *w2, g_out=g_final if l == depth - 1 else None)
    return xt.reshape(b, s, d)
```

```python
import functools
import math

import numpy as np
import jax
import jax.numpy as jnp
from jax import lax
from jax.experimental import pallas as pl
from jax.experimental.pallas import tpu as pltpu

HEAD_DIM = 128
N_MOBA_HEADS = 16
N_SWA_HEADS = 16
N_SWA_KV_HEADS = 4
SWA_GROUP = N_SWA_HEADS // N_SWA_KV_HEADS
MOBA_BLOCK = 256
MOBA_TOPK = 3
SWA_WINDOW = 128
SWA_BLOCK = 128
N_CROSS_HEADS = 4
NUM_BUCKETS = 32
MAX_DISTANCE = 128
EPS = 1e-6

MOBA_W = N_MOBA_HEADS * HEAD_DIM
SWA_W = N_SWA_HEADS * HEAD_DIM
SWA_KV_W = N_SWA_KV_HEADS * HEAD_DIM
CROSS_W = N_CROSS_HEADS * HEAD_DIM

SLAB_QM = 0
SLAB_KM = SLAB_QM + N_MOBA_HEADS
SLAB_VM = SLAB_KM + N_MOBA_HEADS
SLAB_QS = SLAB_VM + N_MOBA_HEADS
SLAB_KS = SLAB_QS + N_SWA_HEADS
SLAB_VS = SLAB_KS + N_SWA_KV_HEADS
N_SLABS = SLAB_VS + N_SWA_KV_HEADS

MASKED = -1e30
LOG2E = math.log2(math.e)
SCORE_SCALE = HEAD_DIM ** -0.5 * LOG2E
LANES = 128
V7X_VMEM_LIMIT_BYTES = 56 * 1024 * 1024

F32 = jnp.float32
BF16 = jnp.bfloat16

_NT = (((1,), (1,)), ((), ()))
_TN = (((0,), (0,)), ((), ()))


def _tile(n, want):
    if n <= want:
        return n
    t = want
    while n % t:
        t -= 8
    return t


def _params(*semantics):
    return pltpu.CompilerParams(dimension_semantics=semantics, vmem_limit_bytes=V7X_VMEM_LIMIT_BYTES)


def _rms_rows(x, g):
    ms = jnp.mean(x * x, axis=-1, keepdims=True)
    return x * lax.rsqrt(ms + EPS) * g


def _rmsnorm_body(x_ref, g_ref, o_ref):
    o_ref[...] = _rms_rows(x_ref[...], g_ref[...]).astype(o_ref.dtype)


def _rmsnorm(x, g, out_dtype):
    m, d = x.shape
    tm = _tile(m, 256)
    return pl.pallas_call(
        _rmsnorm_body,
        grid=(m // tm,),
        in_specs=[pl.BlockSpec((tm, d), lambda i: (i, 0)), pl.BlockSpec((1, d), lambda i: (0, 0))],
        out_specs=pl.BlockSpec((tm, d), lambda i: (i, 0)),
        out_shape=jax.ShapeDtypeStruct((m, d), out_dtype),
        compiler_params=_params("parallel"),
        name="rmsnorm",
    )(x, g.reshape(1, d))


FFN_ROW_CHUNK = 64


def _ffn_body(*refs, n_chunks, final_norm, side_cast):
    x_ref, g_ref, gout_ref, wg_hbm, wu_hbm, wd_hbm = refs[:6]
    refs = refs[6:]
    if side_cast:
        src_hbm, refs = refs[:3], refs[3:]
    o_ref, refs = refs[0], refs[1:]
    if side_cast:
        dst_hbm, refs = refs[:3], refs[3:]
    h_ref, wg_buf, wu_buf, wd_buf, sem = refs[:5]
    if side_cast:
        cin, cout, cin_sem, cout_sem = refs[5:8], refs[8:11], refs[11], refs[12]
    i = pl.program_id(0)
    tm = x_ref.shape[0]
    tf = wd_buf.shape[1]
    last = n_chunks - 1
    backwards = lax.rem(i, 2) == 1

    def fetch(pos, slot):
        cols = pl.ds(pl.multiple_of(jnp.where(backwards, last - pos, pos) * tf, tf), tf)
        return (pltpu.make_async_copy(wg_hbm.at[:, cols], wg_buf.at[slot], sem.at[0, slot]),
                pltpu.make_async_copy(wu_hbm.at[:, cols], wu_buf.at[slot], sem.at[1, slot]),
                pltpu.make_async_copy(wd_hbm.at[cols, :], wd_buf.at[slot], sem.at[2, slot]))

    def start(pos, slot):
        for copy in fetch(pos, slot):
            copy.start()

    def wait(pos, slot):
        for copy in fetch(pos, slot):
            copy.wait()

    def pieces(hbm, pos):
        span = cin[0].shape[1]
        own = pl.ds(pl.multiple_of(i * span, span), span)
        chunk = pl.ds(pl.multiple_of(pos * tf, tf), tf)
        return hbm[0].at[own, chunk], hbm[1].at[own, chunk], hbm[2].at[chunk, own]

    def cast_in(pos, slot):
        return [pltpu.make_async_copy(src, cin[k].at[slot], cin_sem.at[k, slot])
                for k, src in enumerate(pieces(src_hbm, pos))]

    def cast_out(pos, slot):
        return [pltpu.make_async_copy(cout[k].at[slot], dst, cout_sem.at[k, slot])
                for k, dst in enumerate(pieces(dst_hbm, pos))]

    def side(copies, method, when=None):
        if not side_cast:
            return

        def go():
            for copy in copies():
                getattr(copy, method)()

        if when is None:
            go()
        else:
            pl.when(when)(go)

    @pl.when(i == 0)
    def _():
        start(0, 0)

    side(lambda: cast_in(0, 0), "start")

    def norm_rows(r, carry):
        rows = pl.ds(pl.multiple_of(r * FFN_ROW_CHUNK, FFN_ROW_CHUNK), FFN_ROW_CHUNK)
        x = x_ref[rows, :]
        o_ref[rows, :] = x
        inv = lax.rsqrt(jnp.mean(x * x, axis=-1, keepdims=True) + EPS)
        h_ref[rows, :] = (o_ref[rows, :] * inv * g_ref[...]).astype(BF16)
        return carry

    lax.fori_loop(0, tm // FFN_ROW_CHUNK, norm_rows, 0)

    @pl.when(i == 0)
    def _():
        wait(0, 0)

    def compute(slot):
        if side_cast:
            for k in range(3):
                cout[k][slot] = cin[k][slot].astype(BF16)
        h = h_ref[...]
        gate = jnp.dot(h, wg_buf[slot], preferred_element_type=F32)
        up = jnp.dot(h, wu_buf[slot], preferred_element_type=F32)
        act = (0.5 * gate * jax.nn.sigmoid(gate) * up).astype(BF16)
        o_ref[...] += jnp.dot(act, wd_buf[slot], preferred_element_type=F32)

    def chunk_pair(pp, carry):
        pos = 2 * pp
        side(lambda: cast_in(pos, 0), "wait")
        side(lambda: cast_out(pos - 1, 1), "start", pos >= 1)
        side(lambda: cast_out(pos - 2, 0), "wait", pos >= 2)

        @pl.when(pos < last)
        def _():
            start(pos + 1, 1)
            side(lambda: cast_in(pos + 1, 1), "start")

        compute(0)

        @pl.when(pos < last)
        def _():
            wait(pos + 1, 1)
            side(lambda: cast_in(pos + 1, 1), "wait")
            side(lambda: cast_out(pos, 0), "start")
            side(lambda: cast_out(pos - 1, 1), "wait", pos >= 1)
            start(pos + 2, 0)
            side(lambda: cast_in(pos + 2, 0), "start")
            compute(1)
            wait(pos + 2, 0)

        return carry

    lax.fori_loop(0, (n_chunks + 1) // 2, chunk_pair, 0)
    side(lambda: cast_out(last, 0), "start")
    side(lambda: cast_out(last - 1, 1), "wait")
    side(lambda: cast_out(last, 0), "wait")

    if final_norm:
        def out_rows(r, carry):
            rows = pl.ds(pl.multiple_of(r * FFN_ROW_CHUNK, FFN_ROW_CHUNK), FFN_ROW_CHUNK)
            o_ref[rows, :] = _rms_rows(o_ref[rows, :], gout_ref[...])
            return carry

        lax.fori_loop(0, tm // FFN_ROW_CHUNK, out_rows, 0)


def _ffn_tiles(m, d, f):
    tm = _tile(m, 512)
    tf = 256 if f % 256 == 0 else LANES
    return tm, tf


def _can_side_cast(m, d, f):
    tm, tf = _ffn_tiles(m, d, f)
    tiles = m // tm
    return d % tiles == 0 and (d // tiles) % LANES == 0


def _ffn(x, g, wg, wu, wd, g_out=None, cast=None):
    m, d = x.shape
    f = wg.shape[1]
    tm, tf = _ffn_tiles(m, d, f)
    n_chunks = f // tf
    tiles = m // tm
    assert f % tf == 0 and n_chunks % 2 == 1 and tm % FFN_ROW_CHUNK == 0
    final_norm = g_out is not None
    side_cast = cast is not None
    any_spec = pl.BlockSpec(memory_space=pl.ANY)
    in_specs = [
        pl.BlockSpec((tm, d), lambda i: (i, 0)),
        pl.BlockSpec((1, d), lambda i: (0, 0)),
        pl.BlockSpec((1, d), lambda i: (0, 0)),
        any_spec, any_spec, any_spec,
    ]
    out_specs = [pl.BlockSpec((tm, d), lambda i: (i, 0))]
    out_shape = [jax.ShapeDtypeStruct((m, d), F32)]
    scratch = [
        pltpu.VMEM((tm, d), BF16),
        pltpu.VMEM((2, d, tf), BF16),
        pltpu.VMEM((2, d, tf), BF16),
        pltpu.VMEM((2, tf, d), BF16),
        pltpu.SemaphoreType.DMA((3, 2)),
    ]
    operands = [x, g.reshape(1, d), (g_out if final_norm else g).reshape(1, d), wg, wu, wd]
    if side_cast:
        assert _can_side_cast(m, d, f) and all(c.shape == w.shape for c, w in zip(cast, (wg, wu, wd)))
        span = d // tiles
        piece_shapes = [(span, tf), (span, tf), (tf, span)]
        in_specs += [any_spec] * 3
        out_specs += [any_spec] * 3
        out_shape += [jax.ShapeDtypeStruct(c.shape, BF16) for c in cast]
        scratch += [pltpu.VMEM((2,) + s, F32) for s in piece_shapes]
        scratch += [pltpu.VMEM((2,) + s, BF16) for s in piece_shapes]
        scratch += [pltpu.SemaphoreType.DMA((3, 2)), pltpu.SemaphoreType.DMA((3, 2))]
        operands += list(cast)
    body = functools.partial(_ffn_body, n_chunks=n_chunks, final_norm=final_norm, side_cast=side_cast)
    out = pl.pallas_call(
        body,
        grid=(tiles,),
        in_specs=in_specs,
        out_specs=out_specs,
        out_shape=out_shape,
        scratch_shapes=scratch,
        compiler_params=_params("arbitrary"),
        name="ffn",
    )(*operands)
    return out if side_cast else out[0]


def _proj_slabs_body(a_ref, w_ref, b_ref, o_ref):
    acc = jnp.dot(a_ref[...], w_ref[...], preferred_element_type=F32) + b_ref[...]
    for c in range(o_ref.shape[0]):
        o_ref[c] = acc[:, c * LANES:(c + 1) * LANES].astype(o_ref.dtype)


def _proj_slabs(a, w, b):
    m, k = a.shape
    n = w.shape[1]
    tm = _tile(m, 1024)
    tn = _tile(n, 1024)
    return pl.pallas_call(
        _proj_slabs_body,
        grid=(m // tm, n // tn),
        in_specs=[
            pl.BlockSpec((tm, k), lambda i, j: (i, 0)),
            pl.BlockSpec((k, tn), lambda i, j: (0, j)),
            pl.BlockSpec((1, tn), lambda i, j: (0, j)),
        ],
        out_specs=pl.BlockSpec((tn // LANES, tm, LANES), lambda i, j: (j, i, 0)),
        out_shape=jax.ShapeDtypeStruct((n // LANES, m, LANES), BF16),
        compiler_params=_params("parallel", "parallel"),
        name="in_proj",
    )(a, w, b.reshape(1, n))


def _matmul_body(a_ref, w_ref, o_ref):
    o_ref[...] = jnp.dot(a_ref[...], w_ref[...], preferred_element_type=F32).astype(o_ref.dtype)


def _matmul(a, w, out_dtype):
    m, k = a.shape
    n = w.shape[1]
    tm = _tile(m, 1024)
    tn = _tile(n, 512)
    return pl.pallas_call(
        _matmul_body,
        grid=(m // tm, n // tn),
        in_specs=[pl.BlockSpec((tm, k), lambda i, j: (i, 0)), pl.BlockSpec((k, tn), lambda i, j: (0, j))],
        out_specs=pl.BlockSpec((tm, tn), lambda i, j: (i, j)),
        out_shape=jax.ShapeDtypeStruct((m, n), out_dtype),
        compiler_params=_params("parallel", "parallel"),
        name="matmul",
    )(a, w)


def _matmul_residual_body(a_ref, w_ref, r_ref, o_ref):
    o_ref[...] = r_ref[...] + jnp.dot(a_ref[...], w_ref[...], preferred_element_type=F32)


def _matmul_residual(a, w, res):
    m, k = a.shape
    n = w.shape[1]
    tm = _tile(m, 1024)
    tn = _tile(n, 1024)
    return pl.pallas_call(
        _matmul_residual_body,
        grid=(m // tm, n // tn),
        in_specs=[
            pl.BlockSpec((tm, k), lambda i, j: (i, 0)),
            pl.BlockSpec((k, tn), lambda i, j: (0, j)),
            pl.BlockSpec((tm, tn), lambda i, j: (i, j)),
        ],
        out_specs=pl.BlockSpec((tm, tn), lambda i, j: (i, j)),
        out_shape=jax.ShapeDtypeStruct((m, n), F32),
        compiler_params=_params("parallel", "parallel"),
        name="out_proj",
    )(a, w, res)


def _fold_rows(x, op, rows=8):
    while x.shape[0] > rows:
        half = x.shape[0] // 2
        x = op(x[:half], x[half:])
    return x


def _moba_body(q_ref, k_ref, v_ref, town_ref, tprev_ref, bfar_ref, o_ref, km_ref, sel_ref, s_ref, l_ref, acc_ref,
               *, n_blocks, topk, heads):
    n = pl.program_id(2)
    blk = MOBA_BLOCK
    pair = 2 * blk
    scale = SCORE_SCALE
    group = range(heads)

    @pl.when(n == 0)
    def _():
        for g in group:
            for j in range(n_blocks):
                kj = k_ref[g, 0, j * blk:(j + 1) * blk, :].astype(F32)
                km_ref[g, j:j + 1, :] = jnp.mean(kj, axis=0, keepdims=True)

    q = [q_ref[g, 0] for g in group]

    for g in group:
        gate = lax.dot_general(km_ref[g].astype(BF16), q[g], _NT, preferred_element_type=F32)
        rows = lax.broadcasted_iota(jnp.int32, gate.shape, 0)
        gate = jnp.where(rows < n, gate, -jnp.inf)
        sel = jnp.zeros(gate.shape, jnp.bool_)
        for _ in range(topk):
            best = jnp.max(gate, axis=0, keepdims=True)
            first = jnp.min(jnp.where(gate == best, rows, n_blocks), axis=0, keepdims=True)
            pick = (rows == first) & (best > -jnp.inf)
            sel = sel | pick
            gate = jnp.where(pick, -jnp.inf, gate)
        sel_ref[g] = jnp.where(sel, 0.0, MASKED)

    def scores(g, rows):
        return lax.dot_general(k_ref[g, 0, rows, :], q[g], _NT, preferred_element_type=F32) * scale

    def pv(g, rows, p):
        return lax.dot_general(v_ref[g, 0, rows, :], p.astype(BF16), _TN, preferred_element_type=F32)

    jp = jnp.maximum(n - 1, 0)
    own_rows = pl.ds(pl.multiple_of(n * blk, blk), blk)
    prev_rows = pl.ds(pl.multiple_of(jp * blk, blk), blk)
    m8 = []
    for g in group:
        s_own = scores(g, own_rows) + town_ref[g]
        s_prev = scores(g, prev_rows) + tprev_ref[g] + sel_ref[g, pl.ds(jp, 1), :]
        s_ref[g, 0:blk, :] = s_own
        s_ref[g, blk:pair, :] = s_prev
        m8.append(jnp.maximum(_fold_rows(s_own, jnp.maximum), _fold_rows(s_prev, jnp.maximum)))

    n_far = n - 1
    n_pairs = lax.shift_right_logical(n, 1)

    def far_scores(c, m8):
        key_rows = pl.ds(pl.multiple_of(c * pair, pair), pair)
        base = pl.multiple_of(pair + c * pair, pair)
        j1 = 2 * c + 1
        out = []
        for g in group:
            s = scores(g, key_rows)
            bfar = bfar_ref[g]
            s0 = s[:blk] + (sel_ref[g, pl.ds(2 * c, 1), :] + bfar)
            s1 = s[blk:] + jnp.where(j1 < n_far, sel_ref[g, pl.ds(j1, 1), :] + bfar, MASKED)
            s_ref[g, pl.ds(base, blk), :] = s0
            s_ref[g, pl.ds(base + blk, blk), :] = s1
            out.append(jnp.maximum(m8[g], jnp.maximum(_fold_rows(s0, jnp.maximum), _fold_rows(s1, jnp.maximum))))
        return tuple(out)

    m8 = lax.fori_loop(0, n_pairs, far_scores, tuple(m8))
    m = [jnp.max(m8[g], axis=0, keepdims=True) for g in group]

    for g in group:
        p_own = jnp.exp2(s_ref[g, 0:blk, :] - m[g])
        p_prev = jnp.exp2(s_ref[g, blk:pair, :] - m[g])
        l_ref[g] = _fold_rows(p_own, jnp.add) + _fold_rows(p_prev, jnp.add)
        acc_ref[g] = pv(g, own_rows, p_own) + pv(g, prev_rows, p_prev)

    def far_pv(c, carry):
        key_rows = pl.ds(pl.multiple_of(c * pair, pair), pair)
        base = pl.multiple_of(pair + c * pair, pair)
        for g in group:
            p = jnp.exp2(s_ref[g, pl.ds(base, pair), :] - m[g])
            l_ref[g] += _fold_rows(p, jnp.add)
            acc_ref[g] += pv(g, key_rows, p)
        return carry

    lax.fori_loop(0, n_pairs, far_pv, 0)
    for g in group:
        l = jnp.sum(l_ref[g], axis=0, keepdims=True)
        o_ref[g, 0] = (acc_ref[g] / l).T.astype(o_ref.dtype)


def _moba(proj, town, tprev, bfar):
    _, b, s, _ = proj.shape
    blk = MOBA_BLOCK
    n_blocks = s // blk
    heads = 4
    body = functools.partial(_moba_body, n_blocks=n_blocks, topk=min(MOBA_TOPK, n_blocks), heads=heads)
    return pl.pallas_call(
        body,
        grid=(b, N_MOBA_HEADS // heads, n_blocks),
        in_specs=[
            pl.BlockSpec((heads, 1, blk, HEAD_DIM), lambda bi, h, n: (SLAB_QM // heads + h, bi, n, 0)),
            pl.BlockSpec((heads, 1, s, HEAD_DIM), lambda bi, h, n: (SLAB_KM // heads + h, bi, 0, 0)),
            pl.BlockSpec((heads, 1, s, HEAD_DIM), lambda bi, h, n: (SLAB_VM // heads + h, bi, 0, 0)),
            pl.BlockSpec((heads, blk, blk), lambda bi, h, n: (h, 0, 0)),
            pl.BlockSpec((heads, blk, blk), lambda bi, h, n: (h, 0, 0)),
            pl.BlockSpec((heads, 1, blk), lambda bi, h, n: (h, 0, 0)),
        ],
        out_specs=pl.BlockSpec((heads, 1, blk, HEAD_DIM), lambda bi, h, n: (h, bi, n, 0)),
        out_shape=jax.ShapeDtypeStruct((N_MOBA_HEADS, b, s, HEAD_DIM), F32),
        scratch_shapes=[
            pltpu.VMEM((heads, n_blocks, HEAD_DIM), F32),
            pltpu.VMEM((heads, n_blocks, blk), F32),
            pltpu.VMEM((heads, 2 * blk * (1 + (n_blocks - 1) // 2), blk), F32),
            pltpu.VMEM((heads, 8, blk), F32),
            pltpu.VMEM((heads, HEAD_DIM, blk), F32),
        ],
        compiler_params=_params("parallel", "parallel", "arbitrary"),
        name="moba",
    )(proj, proj, proj, town, tprev, bfar)


def _swa_body(q_ref, k_ref, v_ref, bias_ref, sink_ref, o_ref, *, blocks_per_step):
    blk = SWA_BLOCK
    scale = SCORE_SCALE
    sink = sink_ref[0]
    blocks = range(blocks_per_step)

    def band(ref, r):
        n = pl.program_id(2) * blocks_per_step + r
        prev_rows = pl.ds(pl.multiple_of(jnp.maximum(n - 1, 0) * blk, blk), blk)
        own_rows = pl.ds(pl.multiple_of(n * blk, blk), blk)
        return jnp.concatenate([ref[0, 0, prev_rows, :], ref[0, 0, own_rows, :]], axis=0)

    raw = []
    for r in blocks:
        q = jnp.concatenate([q_ref[g, 0, r * blk:(r + 1) * blk, :] for g in range(SWA_GROUP)], axis=0)
        raw.append(lax.dot_general(band(k_ref, r), q, _NT, preferred_element_type=F32))
    soft = []
    for r in blocks:
        n = pl.program_id(2) * blocks_per_step + r
        s = raw[r] * scale + bias_ref[0]
        key_row = lax.broadcasted_iota(jnp.int32, s.shape, 0)
        s = jnp.where((key_row < blk) & (n == 0), MASKED, s)
        m = jnp.maximum(jnp.max(s, axis=0, keepdims=True), sink)
        p = jnp.exp2(s - m)
        soft.append((p.astype(BF16), jnp.sum(p, axis=0, keepdims=True) + jnp.exp2(sink - m)))
    for r in blocks:
        p, l = soft[r]
        o = lax.dot_general(band(v_ref, r), p, _TN, preferred_element_type=F32) / l
        for g in range(SWA_GROUP):
            o_ref[g, 0, r * blk:(r + 1) * blk, :] = o[:, g * blk:(g + 1) * blk].T.astype(o_ref.dtype)


def _swa(proj, bias, sink):
    _, b, s, _ = proj.shape
    blk = SWA_BLOCK
    gq = SWA_GROUP * blk
    per_step = 8 if (s // blk) % 8 == 0 else 1
    rows = per_step * blk
    return pl.pallas_call(
        functools.partial(_swa_body, blocks_per_step=per_step),
        grid=(b, N_SWA_KV_HEADS, s // rows),
        in_specs=[
            pl.BlockSpec((SWA_GROUP, 1, rows, HEAD_DIM), lambda bi, h, n: (SLAB_QS // SWA_GROUP + h, bi, n, 0)),
            pl.BlockSpec((1, 1, s, HEAD_DIM), lambda bi, h, n: (SLAB_KS + h, bi, 0, 0)),
            pl.BlockSpec((1, 1, s, HEAD_DIM), lambda bi, h, n: (SLAB_VS + h, bi, 0, 0)),
            pl.BlockSpec((1, 2 * blk, gq), lambda bi, h, n: (h, 0, 0)),
            pl.BlockSpec((1, 1, gq), lambda bi, h, n: (h, 0, 0)),
        ],
        out_specs=pl.BlockSpec((SWA_GROUP, 1, rows, HEAD_DIM), lambda bi, h, n: (h, bi, n, 0)),
        out_shape=jax.ShapeDtypeStruct((N_SWA_HEADS, b, s, HEAD_DIM), F32),
        compiler_params=_params("parallel", "parallel", "arbitrary"),
        name="swa",
    )(proj, proj, proj, bias, sink)


def _head_norm_body(om_ref, os_ref, gm_ref, gs_ref, o_ref):
    col = 0
    for x_ref, g_ref in ((om_ref, gm_ref), (os_ref, gs_ref)):
        heads = x_ref.shape[0]
        ss = sum(jnp.sum(jnp.square(x_ref[c]), axis=-1, keepdims=True) for c in range(heads))
        r = lax.rsqrt(ss / (heads * HEAD_DIM) + EPS)
        for c in range(heads):
            o_ref[:, col:col + HEAD_DIM] = (x_ref[c] * r * g_ref[:, c * HEAD_DIM:(c + 1) * HEAD_DIM]).astype(o_ref.dtype)
            col += HEAD_DIM


def _head_norm(om, osw, gm, gs):
    hm, m, _ = om.shape
    hs = osw.shape[0]
    tm = _tile(m, 256)
    width = (hm + hs) * HEAD_DIM
    return pl.pallas_call(
        _head_norm_body,
        grid=(m // tm,),
        in_specs=[
            pl.BlockSpec((hm, tm, HEAD_DIM), lambda i: (0, i, 0)),
            pl.BlockSpec((hs, tm, HEAD_DIM), lambda i: (0, i, 0)),
            pl.BlockSpec((1, hm * HEAD_DIM), lambda i: (0, 0)),
            pl.BlockSpec((1, hs * HEAD_DIM), lambda i: (0, 0)),
        ],
        out_specs=pl.BlockSpec((tm, width), lambda i: (i, 0)),
        out_shape=jax.ShapeDtypeStruct((m, width), BF16),
        compiler_params=_params("parallel"),
        name="head_norm",
    )(om, osw, gm.reshape(1, -1), gs.reshape(1, -1))


def _cross_body(x_ref, g_ref, wq_ref, kv_ref, wo_ref, o_ref, h_ref):
    scale = SCORE_SCALE
    tm = x_ref.shape[0]

    def norm_rows(r, carry):
        rows = pl.ds(pl.multiple_of(r * FFN_ROW_CHUNK, FFN_ROW_CHUNK), FFN_ROW_CHUNK)
        h_ref[rows, :] = _rms_rows(x_ref[rows, :], g_ref[...]).astype(BF16)
        return carry

    lax.fori_loop(0, tm // FFN_ROW_CHUNK, norm_rows, 0)
    q = jnp.dot(h_ref[...], wq_ref[...], preferred_element_type=F32).astype(BF16)
    outs = []
    for hd in range(N_CROSS_HEADS):
        cols = slice(hd * HEAD_DIM, (hd + 1) * HEAD_DIM)
        k = kv_ref[0, :, cols]
        v = kv_ref[0, :, CROSS_W + hd * HEAD_DIM:CROSS_W + (hd + 1) * HEAD_DIM]
        s = lax.dot_general(q[:, cols], k, _NT, preferred_element_type=F32) * scale
        p = jnp.exp2(s - jnp.max(s, axis=-1, keepdims=True))
        l = jnp.sum(p, axis=-1, keepdims=True)
        outs.append((jnp.dot(p.astype(BF16), v, preferred_element_type=F32) / l).astype(BF16))
    o = jnp.concatenate(outs, axis=-1)
    o_ref[...] = x_ref[...] + jnp.dot(o, wo_ref[...], preferred_element_type=F32)


def _cross(x, g, wq, kv, wo, seq):
    m, d = x.shape
    n_mem = kv.shape[1]
    tm = _tile(seq, 512)
    assert tm % FFN_ROW_CHUNK == 0
    tiles_per_seq = seq // tm
    resident = pl.Buffered(1)
    return pl.pallas_call(
        _cross_body,
        grid=(m // tm,),
        in_specs=[
            pl.BlockSpec((tm, d), lambda i: (i, 0)),
            pl.BlockSpec((1, d), lambda i: (0, 0)),
            pl.BlockSpec((d, CROSS_W), lambda i: (0, 0), pipeline_mode=resident),
            pl.BlockSpec((1, n_mem, 2 * CROSS_W), lambda i: (i // tiles_per_seq, 0, 0)),
            pl.BlockSpec((CROSS_W, d), lambda i: (0, 0), pipeline_mode=resident),
        ],
        out_specs=pl.BlockSpec((tm, d), lambda i: (i, 0)),
        out_shape=jax.ShapeDtypeStruct((m, d), F32),
        scratch_shapes=[pltpu.VMEM((tm, d), BF16)],
        compiler_params=_params("parallel"),
        name="cross",
    )(x, g.reshape(1, d), wq, kv, wo)


def _t5_bucket(dist):
    n = np.maximum(dist, 0)
    max_exact = NUM_BUCKETS // 2
    nf = np.maximum(n, max_exact).astype(np.float32)
    large = max_exact + (np.log(nf / np.float32(max_exact)) / np.float32(math.log(MAX_DISTANCE / max_exact))
                         * np.float32(NUM_BUCKETS - max_exact)).astype(np.int32)
    large = np.minimum(large, NUM_BUCKETS - 1)
    return np.where(n < max_exact, n, large).astype(np.int32)


def _bias_by_distance(bias_hb, dists):
    onehot = _t5_bucket(dists)[None, :] == np.arange(NUM_BUCKETS)[:, None]
    return jnp.sum(jnp.where(onehot[None], bias_hb[:, :, None], 0.0), axis=1)


def _toeplitz(g, rows, cols):
    heads, n = g.shape
    assert n == rows + cols - 1
    width = rows + cols
    tiled = jnp.broadcast_to(jnp.pad(g, ((0, 0), (0, 1)))[:, None, :], (heads, rows, width))
    skew = tiled.reshape(heads, rows * width)[:, :rows * (width - 1)].reshape(heads, rows, width - 1)
    return skew[:, :, rows - 1:rows - 1 + cols]


def _moba_bias_tables(bias_hb):
    blk = MOBA_BLOCK
    d_own = np.arange(2 * blk - 1) - (blk - 1)
    g_own = jnp.where(d_own >= 0, _bias_by_distance(bias_hb, np.maximum(d_own, 0)), MASKED)
    g_prev = _bias_by_distance(bias_hb, d_own + blk)
    far = int(_t5_bucket(np.array(blk + 1)))
    bfar = jnp.broadcast_to(bias_hb[:, far][:, None, None], (bias_hb.shape[0], 1, blk))
    return _toeplitz(g_own, blk, blk), _toeplitz(g_prev, blk, blk), bfar


def _swa_bias_table(bias_hb):
    blk = SWA_BLOCK
    dist = np.arange(3 * blk - 1) - (blk - 1)
    ok = (dist >= 0) & (dist < SWA_WINDOW)
    g = jnp.where(ok, _bias_by_distance(bias_hb, np.maximum(dist, 0)), MASKED)
    t = _toeplitz(g, 2 * blk, blk)
    t = t.reshape(N_SWA_KV_HEADS, SWA_GROUP, 2 * blk, blk).transpose(0, 2, 1, 3)
    return t.reshape(N_SWA_KV_HEADS, 2 * blk, SWA_GROUP * blk)


def kernel(x, mem, rel_bias, g_final, g_ffn1, w1_gate, w1_up, w1_down, g_mix, w_in, b_in, sinks, g_out_moba, g_out_swa, w_out, g_cross, g_mem, w_cq, w_ck, w_cv, w_co, g_ffn2, w2_gate, w2_up, w2_down):
    b, s, d = x.shape
    n_mem = mem.shape[1]
    depth = w_in.shape[0]
    assert s % MOBA_BLOCK == 0 and s % SWA_BLOCK == 0 and d % LANES == 0

    bias_moba = rel_bias[:, :N_MOBA_HEADS].T.astype(F32)
    bias_swa = rel_bias[:, N_MOBA_HEADS:].T.astype(F32)
    town, tprev, bfar = _moba_bias_tables(bias_moba * LOG2E)
    tswa = _swa_bias_table(bias_swa * LOG2E)

    xt = x.reshape(b * s, d)
    memt = mem.reshape(b * n_mem, d)
    for l in range(depth):
        w1 = [w.astype(BF16) for w in (w1_gate[l], w1_up[l], w1_down[l])]
        w2 = (w2_gate[l], w2_up[l], w2_down[l])
        if _can_side_cast(b * s, d, w1[0].shape[1]) and all(p.shape == q.shape for p, q in zip(w1, w2)):
            xt, *w2 = _ffn(xt, g_ffn1[l], *w1, cast=w2)
        else:
            xt = _ffn(xt, g_ffn1[l], *w1)
            w2 = [w.astype(BF16) for w in w2]

        nrm = _rmsnorm(xt, g_mix[l], BF16)
        proj = _proj_slabs(nrm, w_in[l].astype(BF16), b_in[l]).reshape(N_SLABS, b, s, HEAD_DIM)
        o_m = _moba(proj, town, tprev, bfar).reshape(N_MOBA_HEADS, b * s, HEAD_DIM)
        sink = jnp.broadcast_to((sinks[l].astype(F32) * LOG2E).reshape(N_SWA_KV_HEADS, SWA_GROUP, 1),
                                (N_SWA_KV_HEADS, SWA_GROUP, SWA_BLOCK)).reshape(N_SWA_KV_HEADS, 1, SWA_GROUP * SWA_BLOCK)
        o_s = _swa(proj, tswa, sink).reshape(N_SWA_HEADS, b * s, HEAD_DIM)
        o = _head_norm(o_m, o_s, g_out_moba[l], g_out_swa[l])
        xt = _matmul_residual(o, w_out[l].astype(BF16), xt)

        mem_n = _rmsnorm(memt, g_mem[l], BF16)
        w_kv = jnp.concatenate([w_ck[l], w_cv[l]], axis=1).astype(BF16)
        kv = _matmul(mem_n, w_kv, BF16).reshape(b, n_mem, 2 * CROSS_W)
        xt = _cross(xt, g_cross[l], w_cq[l].astype(BF16), kv, w_co[l].astype(BF16), s)

        xt = _ffn(xt, g_ffn2[l], *w2, g_out=g_final if l == depth - 1 else None)
    return xt.reshape(b, s, d)
```

```python
import functools
import math

import numpy as np
import jax
import jax.numpy as jnp
from jax import lax
from jax.experimental import pallas as pl
from jax.experimental.pallas import tpu as pltpu

HEAD_DIM = 128
N_MOBA_HEADS = 16
N_SWA_HEADS = 16
N_SWA_KV_HEADS = 4
SWA_GROUP = N_SWA_HEADS // N_SWA_KV_HEADS
MOBA_BLOCK = 256
MOBA_TOPK = 3
SWA_WINDOW = 128
SWA_BLOCK = 128
N_CROSS_HEADS = 4
NUM_BUCKETS = 32
MAX_DISTANCE = 128
EPS = 1e-6

MOBA_W = N_MOBA_HEADS * HEAD_DIM
SWA_W = N_SWA_HEADS * HEAD_DIM
SWA_KV_W = N_SWA_KV_HEADS * HEAD_DIM
CROSS_W = N_CROSS_HEADS * HEAD_DIM

SLAB_QM = 0
SLAB_KM = SLAB_QM + N_MOBA_HEADS
SLAB_VM = SLAB_KM + N_MOBA_HEADS
SLAB_QS = SLAB_VM + N_MOBA_HEADS
SLAB_KS = SLAB_QS + N_SWA_HEADS
SLAB_VS = SLAB_KS + N_SWA_KV_HEADS
N_SLABS = SLAB_VS + N_SWA_KV_HEADS

MASKED = -1e30
LOG2E = math.log2(math.e)
SCORE_SCALE = HEAD_DIM ** -0.5 * LOG2E
LANES = 128
V7X_VMEM_LIMIT_BYTES = 56 * 1024 * 1024

F32 = jnp.float32
BF16 = jnp.bfloat16

_NT = (((1,), (1,)), ((), ()))
_TN = (((0,), (0,)), ((), ()))


def _tile(n, want):
    if n <= want:
        return n
    t = want
    while n % t:
        t -= 8
    return t


def _params(*semantics):
    return pltpu.CompilerParams(dimension_semantics=semantics, vmem_limit_bytes=V7X_VMEM_LIMIT_BYTES)


def _rms_rows(x, g):
    ms = jnp.mean(x * x, axis=-1, keepdims=True)
    return x * lax.rsqrt(ms + EPS) * g


def _rmsnorm_body(x_ref, g_ref, o_ref):
    o_ref[...] = _rms_rows(x_ref[...], g_ref[...]).astype(o_ref.dtype)


def _rmsnorm(x, g, out_dtype):
    m, d = x.shape
    tm = _tile(m, 256)
    return pl.pallas_call(
        _rmsnorm_body,
        grid=(m // tm,),
        in_specs=[pl.BlockSpec((tm, d), lambda i: (i, 0)), pl.BlockSpec((1, d), lambda i: (0, 0))],
        out_specs=pl.BlockSpec((tm, d), lambda i: (i, 0)),
        out_shape=jax.ShapeDtypeStruct((m, d), out_dtype),
        compiler_params=_params("parallel"),
        name="rmsnorm",
    )(x, g.reshape(1, d))


FFN_ROW_CHUNK = 64


def _ffn_body(*refs, n_chunks, final_norm, side_cast):
    x_hbm, g_ref, gout_ref, wg_hbm, wu_hbm, wd_hbm = refs[:6]
    refs = refs[6:]
    if side_cast:
        src_hbm, refs = refs[:3], refs[3:]
    o_hbm, refs = refs[0], refs[1:]
    if side_cast:
        dst_hbm, refs = refs[:3], refs[3:]
    tile_buf, h_ref, wg_buf, wu_buf, wd_buf, sem, x_sem, o_sem = refs[:8]
    if side_cast:
        cin, cout, cin_sem, cout_sem = refs[8:11], refs[11:14], refs[14], refs[15]
    i = pl.program_id(0)
    n_tiles = pl.num_programs(0)
    tm = tile_buf.shape[1]
    tf = wd_buf.shape[1]
    last = n_chunks - 1
    backwards = lax.rem(i, 2) == 1
    cur = lax.rem(i, 2)
    o_ref = tile_buf.at[cur]

    def tile_rows(t):
        return pl.ds(pl.multiple_of(t * tm, tm), tm)

    def x_in(t, slot):
        return pltpu.make_async_copy(x_hbm.at[tile_rows(t), :], tile_buf.at[slot], x_sem.at[slot])

    def o_out(t, slot):
        return pltpu.make_async_copy(tile_buf.at[slot], o_hbm.at[tile_rows(t), :], o_sem.at[slot])

    @pl.when(i == 0)
    def _():
        x_in(0, 0).start()

    def fetch(pos, slot):
        cols = pl.ds(pl.multiple_of(jnp.where(backwards, last - pos, pos) * tf, tf), tf)
        return (pltpu.make_async_copy(wg_hbm.at[:, cols], wg_buf.at[slot], sem.at[0, slot]),
                pltpu.make_async_copy(wu_hbm.at[:, cols], wu_buf.at[slot], sem.at[1, slot]),
                pltpu.make_async_copy(wd_hbm.at[cols, :], wd_buf.at[slot], sem.at[2, slot]))

    def start(pos, slot):
        for copy in fetch(pos, slot):
            copy.start()

    def wait(pos, slot):
        for copy in fetch(pos, slot):
            copy.wait()

    def pieces(hbm, pos):
        span = cin[0].shape[1]
        own = pl.ds(pl.multiple_of(i * span, span), span)
        chunk = pl.ds(pl.multiple_of(pos * tf, tf), tf)
        return hbm[0].at[own, chunk], hbm[1].at[own, chunk], hbm[2].at[chunk, own]

    def cast_in(pos, slot):
        return [pltpu.make_async_copy(src, cin[k].at[slot], cin_sem.at[k, slot])
                for k, src in enumerate(pieces(src_hbm, pos))]

    def cast_out(pos, slot):
        return [pltpu.make_async_copy(cout[k].at[slot], dst, cout_sem.at[k, slot])
                for k, dst in enumerate(pieces(dst_hbm, pos))]

    def side(copies, method, when=None):
        if not side_cast:
            return

        def go():
            for copy in copies():
                getattr(copy, method)()

        if when is None:
            go()
        else:
            pl.when(when)(go)

    @pl.when(i == 0)
    def _():
        start(0, 0)

    side(lambda: cast_in(0, 0), "start")
    x_in(i, cur).wait()

    def norm_rows(r, carry):
        rows = pl.ds(pl.multiple_of(r * FFN_ROW_CHUNK, FFN_ROW_CHUNK), FFN_ROW_CHUNK)
        h_ref[rows, :] = _rms_rows(o_ref[rows, :], g_ref[...]).astype(BF16)
        return carry

    lax.fori_loop(0, tm // FFN_ROW_CHUNK, norm_rows, 0)

    @pl.when(i == 0)
    def _():
        wait(0, 0)

    def compute(slot):
        if side_cast:
            for k in range(3):
                cout[k][slot] = cin[k][slot].astype(BF16)
        h = h_ref[...]
        gate = jnp.dot(h, wg_buf[slot], preferred_element_type=F32)
        up = jnp.dot(h, wu_buf[slot], preferred_element_type=F32)
        act = (0.5 * gate * jax.nn.sigmoid(gate) * up).astype(BF16)
        o_ref[...] += jnp.dot(act, wd_buf[slot], preferred_element_type=F32)

    def chunk_pair(pp, carry):
        pos = 2 * pp

        @pl.when((pp == 1) & (i >= 1))
        def _():
            o_out(i - 1, 1 - cur).wait()

        @pl.when((pp == 1) & (i + 1 < n_tiles))
        def _():
            x_in(i + 1, 1 - cur).start()

        side(lambda: cast_in(pos, 0), "wait")
        side(lambda: cast_out(pos - 1, 1), "start", pos >= 1)
        side(lambda: cast_out(pos - 2, 0), "wait", pos >= 2)

        @pl.when(pos < last)
        def _():
            start(pos + 1, 1)
            side(lambda: cast_in(pos + 1, 1), "start")

        compute(0)

        @pl.when(pos < last)
        def _():
            wait(pos + 1, 1)
            side(lambda: cast_in(pos + 1, 1), "wait")
            side(lambda: cast_out(pos, 0), "start")
            side(lambda: cast_out(pos - 1, 1), "wait", pos >= 1)
            start(pos + 2, 0)
            side(lambda: cast_in(pos + 2, 0), "start")
            compute(1)
            wait(pos + 2, 0)

        return carry

    lax.fori_loop(0, (n_chunks + 1) // 2, chunk_pair, 0)
    side(lambda: cast_out(last, 0), "start")
    side(lambda: cast_out(last - 1, 1), "wait")
    side(lambda: cast_out(last, 0), "wait")

    if final_norm:
        def out_rows(r, carry):
            rows = pl.ds(pl.multiple_of(r * FFN_ROW_CHUNK, FFN_ROW_CHUNK), FFN_ROW_CHUNK)
            o_ref[rows, :] = _rms_rows(o_ref[rows, :], gout_ref[...])
            return carry

        lax.fori_loop(0, tm // FFN_ROW_CHUNK, out_rows, 0)

    o_out(i, cur).start()

    @pl.when(i == n_tiles - 1)
    def _():
        o_out(i, cur).wait()


def _ffn_tiles(m, d, f):
    tm = _tile(m, 1024)
    tf = 256 if f % 256 == 0 else LANES
    return tm, tf


def _can_side_cast(m, d, f):
    tm, tf = _ffn_tiles(m, d, f)
    tiles = m // tm
    return d % tiles == 0 and (d // tiles) % LANES == 0


def _ffn(x, g, wg, wu, wd, g_out=None, cast=None):
    m, d = x.shape
    f = wg.shape[1]
    tm, tf = _ffn_tiles(m, d, f)
    n_chunks = f // tf
    tiles = m // tm
    assert f % tf == 0 and n_chunks % 2 == 1 and n_chunks >= 3 and tm % FFN_ROW_CHUNK == 0
    final_norm = g_out is not None
    side_cast = cast is not None
    any_spec = pl.BlockSpec(memory_space=pl.ANY)
    in_specs = [
        any_spec,
        pl.BlockSpec((1, d), lambda i: (0, 0)),
        pl.BlockSpec((1, d), lambda i: (0, 0)),
        any_spec, any_spec, any_spec,
    ]
    out_specs = [any_spec]
    out_shape = [jax.ShapeDtypeStruct((m, d), F32)]
    scratch = [
        pltpu.VMEM((2, tm, d), F32),
        pltpu.VMEM((tm, d), BF16),
        pltpu.VMEM((2, d, tf), BF16),
        pltpu.VMEM((2, d, tf), BF16),
        pltpu.VMEM((2, tf, d), BF16),
        pltpu.SemaphoreType.DMA((3, 2)),
        pltpu.SemaphoreType.DMA((2,)),
        pltpu.SemaphoreType.DMA((2,)),
    ]
    operands = [x, g.reshape(1, d), (g_out if final_norm else g).reshape(1, d), wg, wu, wd]
    if side_cast:
        assert _can_side_cast(m, d, f) and all(c.shape == w.shape for c, w in zip(cast, (wg, wu, wd)))
        span = d // tiles
        piece_shapes = [(span, tf), (span, tf), (tf, span)]
        in_specs += [any_spec] * 3
        out_specs += [any_spec] * 3
        out_shape += [jax.ShapeDtypeStruct(c.shape, BF16) for c in cast]
        scratch += [pltpu.VMEM((2,) + s, F32) for s in piece_shapes]
        scratch += [pltpu.VMEM((2,) + s, BF16) for s in piece_shapes]
        scratch += [pltpu.SemaphoreType.DMA((3, 2)), pltpu.SemaphoreType.DMA((3, 2))]
        operands += list(cast)
    body = functools.partial(_ffn_body, n_chunks=n_chunks, final_norm=final_norm, side_cast=side_cast)
    out = pl.pallas_call(
        body,
        grid=(tiles,),
        in_specs=in_specs,
        out_specs=out_specs,
        out_shape=out_shape,
        scratch_shapes=scratch,
        compiler_params=_params("arbitrary"),
        name="ffn",
    )(*operands)
    return out if side_cast else out[0]


def _proj_slabs_body(a_ref, w_ref, b_ref, o_ref):
    acc = jnp.dot(a_ref[...], w_ref[...], preferred_element_type=F32) + b_ref[...]
    for c in range(o_ref.shape[0]):
        o_ref[c] = acc[:, c * LANES:(c + 1) * LANES].astype(o_ref.dtype)


def _proj_slabs(a, w, b):
    m, k = a.shape
    n = w.shape[1]
    tm = _tile(m, 1024)
    tn = _tile(n, 1024)
    return pl.pallas_call(
        _proj_slabs_body,
        grid=(m // tm, n // tn),
        in_specs=[
            pl.BlockSpec((tm, k), lambda i, j: (i, 0)),
            pl.BlockSpec((k, tn), lambda i, j: (0, j)),
            pl.BlockSpec((1, tn), lambda i, j: (0, j)),
        ],
        out_specs=pl.BlockSpec((tn // LANES, tm, LANES), lambda i, j: (j, i, 0)),
        out_shape=jax.ShapeDtypeStruct((n // LANES, m, LANES), BF16),
        compiler_params=_params("parallel", "parallel"),
        name="in_proj",
    )(a, w, b.reshape(1, n))


def _matmul_body(a_ref, w_ref, o_ref):
    o_ref[...] = jnp.dot(a_ref[...], w_ref[...], preferred_element_type=F32).astype(o_ref.dtype)


def _matmul(a, w, out_dtype):
    m, k = a.shape
    n = w.shape[1]
    tm = _tile(m, 1024)
    tn = _tile(n, 512)
    return pl.pallas_call(
        _matmul_body,
        grid=(m // tm, n // tn),
        in_specs=[pl.BlockSpec((tm, k), lambda i, j: (i, 0)), pl.BlockSpec((k, tn), lambda i, j: (0, j))],
        out_specs=pl.BlockSpec((tm, tn), lambda i, j: (i, j)),
        out_shape=jax.ShapeDtypeStruct((m, n), out_dtype),
        compiler_params=_params("parallel", "parallel"),
        name="matmul",
    )(a, w)


def _matmul_residual_body(a_ref, w_ref, r_ref, o_ref):
    o_ref[...] = r_ref[...] + jnp.dot(a_ref[...], w_ref[...], preferred_element_type=F32)


def _matmul_residual(a, w, res):
    m, k = a.shape
    n = w.shape[1]
    tm = _tile(m, 1024)
    tn = _tile(n, 1024)
    return pl.pallas_call(
        _matmul_residual_body,
        grid=(m // tm, n // tn),
        in_specs=[
            pl.BlockSpec((tm, k), lambda i, j: (i, 0)),
            pl.BlockSpec((k, tn), lambda i, j: (0, j)),
            pl.BlockSpec((tm, tn), lambda i, j: (i, j)),
        ],
        out_specs=pl.BlockSpec((tm, tn), lambda i, j: (i, j)),
        out_shape=jax.ShapeDtypeStruct((m, n), F32),
        compiler_params=_params("parallel", "parallel"),
        name="out_proj",
    )(a, w, res)


def _fold_rows(x, op, rows=8):
    while x.shape[0] > rows:
        half = x.shape[0] // 2
        x = op(x[:half], x[half:])
    return x


def _moba_body(q_ref, k_ref, v_ref, town_ref, tprev_ref, bfar_ref, o_ref, km_ref, sel_ref, s_ref, l_ref, acc_ref,
               *, n_blocks, topk, heads):
    n = pl.program_id(2)
    blk = MOBA_BLOCK
    pair = 2 * blk
    scale = SCORE_SCALE
    group = range(heads)

    @pl.when(n == 0)
    def _():
        for g in group:
            for j in range(n_blocks):
                kj = k_ref[g, 0, j * blk:(j + 1) * blk, :].astype(F32)
                km_ref[g, j:j + 1, :] = jnp.mean(kj, axis=0, keepdims=True)

    q = [q_ref[g, 0] for g in group]

    for g in group:
        gate = lax.dot_general(km_ref[g].astype(BF16), q[g], _NT, preferred_element_type=F32)
        rows = lax.broadcasted_iota(jnp.int32, gate.shape, 0)
        gate = jnp.where(rows < n, gate, -jnp.inf)
        sel = jnp.zeros(gate.shape, jnp.bool_)
        for _ in range(topk):
            best = jnp.max(gate, axis=0, keepdims=True)
            first = jnp.min(jnp.where(gate == best, rows, n_blocks), axis=0, keepdims=True)
            pick = (rows == first) & (best > -jnp.inf)
            sel = sel | pick
            gate = jnp.where(pick, -jnp.inf, gate)
        sel_ref[g] = jnp.where(sel, 0.0, MASKED)

    def scores(g, rows):
        return lax.dot_general(k_ref[g, 0, rows, :], q[g], _NT, preferred_element_type=F32) * scale

    def pv(g, rows, p):
        return lax.dot_general(v_ref[g, 0, rows, :], p.astype(BF16), _TN, preferred_element_type=F32)

    jp = jnp.maximum(n - 1, 0)
    own_rows = pl.ds(pl.multiple_of(n * blk, blk), blk)
    prev_rows = pl.ds(pl.multiple_of(jp * blk, blk), blk)
    m8 = []
    for g in group:
        s_own = scores(g, own_rows) + town_ref[g]
        s_prev = scores(g, prev_rows) + tprev_ref[g] + sel_ref[g, pl.ds(jp, 1), :]
        s_ref[g, 0:blk, :] = s_own
        s_ref[g, blk:pair, :] = s_prev
        m8.append(jnp.maximum(_fold_rows(s_own, jnp.maximum), _fold_rows(s_prev, jnp.maximum)))

    n_far = n - 1
    n_pairs = lax.shift_right_logical(n, 1)

    def far_scores(c, m8):
        key_rows = pl.ds(pl.multiple_of(c * pair, pair), pair)
        base = pl.multiple_of(pair + c * pair, pair)
        j1 = 2 * c + 1
        out = []
        for g in group:
            s = scores(g, key_rows)
            bfar = bfar_ref[g]
            s0 = s[:blk] + (sel_ref[g, pl.ds(2 * c, 1), :] + bfar)
            s1 = s[blk:] + jnp.where(j1 < n_far, sel_ref[g, pl.ds(j1, 1), :] + bfar, MASKED)
            s_ref[g, pl.ds(base, blk), :] = s0
            s_ref[g, pl.ds(base + blk, blk), :] = s1
            out.append(jnp.maximum(m8[g], jnp.maximum(_fold_rows(s0, jnp.maximum), _fold_rows(s1, jnp.maximum))))
        return tuple(out)

    m8 = lax.fori_loop(0, n_pairs, far_scores, tuple(m8))
    m = [jnp.max(m8[g], axis=0, keepdims=True) for g in group]

    for g in group:
        p_own = jnp.exp2(s_ref[g, 0:blk, :] - m[g])
        p_prev = jnp.exp2(s_ref[g, blk:pair, :] - m[g])
        l_ref[g] = _fold_rows(p_own, jnp.add) + _fold_rows(p_prev, jnp.add)
        acc_ref[g] = pv(g, own_rows, p_own) + pv(g, prev_rows, p_prev)

    def far_pv(c, carry):
        key_rows = pl.ds(pl.multiple_of(c * pair, pair), pair)
        base = pl.multiple_of(pair + c * pair, pair)
        for g in group:
            p = jnp.exp2(s_ref[g, pl.ds(base, pair), :] - m[g])
            l_ref[g] += _fold_rows(p, jnp.add)
            acc_ref[g] += pv(g, key_rows, p)
        return carry

    lax.fori_loop(0, n_pairs, far_pv, 0)
    for g in group:
        l = jnp.sum(l_ref[g], axis=0, keepdims=True)
        o_ref[g, 0] = (acc_ref[g] / l).T.astype(o_ref.dtype)


def _moba(proj, town, tprev, bfar):
    _, b, s, _ = proj.shape
    blk = MOBA_BLOCK
    n_blocks = s // blk
    heads = 4
    body = functools.partial(_moba_body, n_blocks=n_blocks, topk=min(MOBA_TOPK, n_blocks), heads=heads)
    return pl.pallas_call(
        body,
        grid=(b, N_MOBA_HEADS // heads, n_blocks),
        in_specs=[
            pl.BlockSpec((heads, 1, blk, HEAD_DIM), lambda bi, h, n: (SLAB_QM // heads + h, bi, n, 0)),
            pl.BlockSpec((heads, 1, s, HEAD_DIM), lambda bi, h, n: (SLAB_KM // heads + h, bi, 0, 0)),
            pl.BlockSpec((heads, 1, s, HEAD_DIM), lambda bi, h, n: (SLAB_VM // heads + h, bi, 0, 0)),
            pl.BlockSpec((heads, blk, blk), lambda bi, h, n: (h, 0, 0)),
            pl.BlockSpec((heads, blk, blk), lambda bi, h, n: (h, 0, 0)),
            pl.BlockSpec((heads, 1, blk), lambda bi, h, n: (h, 0, 0)),
        ],
        out_specs=pl.BlockSpec((heads, 1, blk, HEAD_DIM), lambda bi, h, n: (h, bi, n, 0)),
        out_shape=jax.ShapeDtypeStruct((N_MOBA_HEADS, b, s, HEAD_DIM), F32),
        scratch_shapes=[
            pltpu.VMEM((heads, n_blocks, HEAD_DIM), F32),
            pltpu.VMEM((heads, n_blocks, blk), F32),
            pltpu.VMEM((heads, 2 * blk * (1 + (n_blocks - 1) // 2), blk), F32),
            pltpu.VMEM((heads, 8, blk), F32),
            pltpu.VMEM((heads, HEAD_DIM, blk), F32),
        ],
        compiler_params=_params("parallel", "parallel", "arbitrary"),
        name="moba",
    )(proj, proj, proj, town, tprev, bfar)


def _swa_body(q_ref, k_ref, v_ref, bias_ref, sink_ref, o_ref, *, blocks_per_step):
    blk = SWA_BLOCK
    scale = SCORE_SCALE
    sink = sink_ref[0]
    blocks = range(blocks_per_step)

    def band(ref, r):
        n = pl.program_id(2) * blocks_per_step + r
        prev_rows = pl.ds(pl.multiple_of(jnp.maximum(n - 1, 0) * blk, blk), blk)
        own_rows = pl.ds(pl.multiple_of(n * blk, blk), blk)
        return jnp.concatenate([ref[0, 0, prev_rows, :], ref[0, 0, own_rows, :]], axis=0)

    raw = []
    for r in blocks:
        q = jnp.concatenate([q_ref[g, 0, r * blk:(r + 1) * blk, :] for g in range(SWA_GROUP)], axis=0)
        raw.append(lax.dot_general(band(k_ref, r), q, _NT, preferred_element_type=F32))
    soft = []
    for r in blocks:
        n = pl.program_id(2) * blocks_per_step + r
        s = raw[r] * scale + bias_ref[0]
        key_row = lax.broadcasted_iota(jnp.int32, s.shape, 0)
        s = jnp.where((key_row < blk) & (n == 0), MASKED, s)
        m = jnp.maximum(jnp.max(s, axis=0, keepdims=True), sink)
        p = jnp.exp2(s - m)
        soft.append((p.astype(BF16), jnp.sum(p, axis=0, keepdims=True) + jnp.exp2(sink - m)))
    for r in blocks:
        p, l = soft[r]
        o = lax.dot_general(band(v_ref, r), p, _TN, preferred_element_type=F32) / l
        for g in range(SWA_GROUP):
            o_ref[g, 0, r * blk:(r + 1) * blk, :] = o[:, g * blk:(g + 1) * blk].T.astype(o_ref.dtype)


def _swa(proj, bias, sink):
    _, b, s, _ = proj.shape
    blk = SWA_BLOCK
    gq = SWA_GROUP * blk
    per_step = 8 if (s // blk) % 8 == 0 else 1
    rows = per_step * blk
    return pl.pallas_call(
        functools.partial(_swa_body, blocks_per_step=per_step),
        grid=(b, N_SWA_KV_HEADS, s // rows),
        in_specs=[
            pl.BlockSpec((SWA_GROUP, 1, rows, HEAD_DIM), lambda bi, h, n: (SLAB_QS // SWA_GROUP + h, bi, n, 0)),
            pl.BlockSpec((1, 1, s, HEAD_DIM), lambda bi, h, n: (SLAB_KS + h, bi, 0, 0)),
            pl.BlockSpec((1, 1, s, HEAD_DIM), lambda bi, h, n: (SLAB_VS + h, bi, 0, 0)),
            pl.BlockSpec((1, 2 * blk, gq), lambda bi, h, n: (h, 0, 0)),
            pl.BlockSpec((1, 1, gq), lambda bi, h, n: (h, 0, 0)),
        ],
        out_specs=pl.BlockSpec((SWA_GROUP, 1, rows, HEAD_DIM), lambda bi, h, n: (h, bi, n, 0)),
        out_shape=jax.ShapeDtypeStruct((N_SWA_HEADS, b, s, HEAD_DIM), F32),
        compiler_params=_params("parallel", "parallel", "arbitrary"),
        name="swa",
    )(proj, proj, proj, bias, sink)


def _head_norm_body(om_ref, os_ref, gm_ref, gs_ref, o_ref):
    col = 0
    for x_ref, g_ref in ((om_ref, gm_ref), (os_ref, gs_ref)):
        heads = x_ref.shape[0]
        ss = sum(jnp.sum(jnp.square(x_ref[c]), axis=-1, keepdims=True) for c in range(heads))
        r = lax.rsqrt(ss / (heads * HEAD_DIM) + EPS)
        for c in range(heads):
            o_ref[:, col:col + HEAD_DIM] = (x_ref[c] * r * g_ref[:, c * HEAD_DIM:(c + 1) * HEAD_DIM]).astype(o_ref.dtype)
            col += HEAD_DIM


def _head_norm(om, osw, gm, gs):
    hm, m, _ = om.shape
    hs = osw.shape[0]
    tm = _tile(m, 256)
    width = (hm + hs) * HEAD_DIM
    return pl.pallas_call(
        _head_norm_body,
        grid=(m // tm,),
        in_specs=[
            pl.BlockSpec((hm, tm, HEAD_DIM), lambda i: (0, i, 0)),
            pl.BlockSpec((hs, tm, HEAD_DIM), lambda i: (0, i, 0)),
            pl.BlockSpec((1, hm * HEAD_DIM), lambda i: (0, 0)),
            pl.BlockSpec((1, hs * HEAD_DIM), lambda i: (0, 0)),
        ],
        out_specs=pl.BlockSpec((tm, width), lambda i: (i, 0)),
        out_shape=jax.ShapeDtypeStruct((m, width), BF16),
        compiler_params=_params("parallel"),
        name="head_norm",
    )(om, osw, gm.reshape(1, -1), gs.reshape(1, -1))


def _cross_body(x_ref, g_ref, wq_ref, kv_ref, wo_ref, o_ref, h_ref):
    scale = SCORE_SCALE
    tm = x_ref.shape[0]

    def norm_rows(r, carry):
        rows = pl.ds(pl.multiple_of(r * FFN_ROW_CHUNK, FFN_ROW_CHUNK), FFN_ROW_CHUNK)
        h_ref[rows, :] = _rms_rows(x_ref[rows, :], g_ref[...]).astype(BF16)
        return carry

    lax.fori_loop(0, tm // FFN_ROW_CHUNK, norm_rows, 0)
    q = jnp.dot(h_ref[...], wq_ref[...], preferred_element_type=F32).astype(BF16)
    outs = []
    for hd in range(N_CROSS_HEADS):
        cols = slice(hd * HEAD_DIM, (hd + 1) * HEAD_DIM)
        k = kv_ref[0, :, cols]
        v = kv_ref[0, :, CROSS_W + hd * HEAD_DIM:CROSS_W + (hd + 1) * HEAD_DIM]
        s = lax.dot_general(q[:, cols], k, _NT, preferred_element_type=F32) * scale
        p = jnp.exp2(s - jnp.max(s, axis=-1, keepdims=True))
        l = jnp.sum(p, axis=-1, keepdims=True)
        outs.append((jnp.dot(p.astype(BF16), v, preferred_element_type=F32) / l).astype(BF16))
    o = jnp.concatenate(outs, axis=-1)
    o_ref[...] = x_ref[...] + jnp.dot(o, wo_ref[...], preferred_element_type=F32)


def _cross(x, g, wq, kv, wo, seq):
    m, d = x.shape
    n_mem = kv.shape[1]
    tm = _tile(seq, 512)
    assert tm % FFN_ROW_CHUNK == 0
    tiles_per_seq = seq // tm
    resident = pl.Buffered(1)
    return pl.pallas_call(
        _cross_body,
        grid=(m // tm,),
        in_specs=[
            pl.BlockSpec((tm, d), lambda i: (i, 0)),
            pl.BlockSpec((1, d), lambda i: (0, 0)),
            pl.BlockSpec((d, CROSS_W), lambda i: (0, 0), pipeline_mode=resident),
            pl.BlockSpec((1, n_mem, 2 * CROSS_W), lambda i: (i // tiles_per_seq, 0, 0)),
            pl.BlockSpec((CROSS_W, d), lambda i: (0, 0), pipeline_mode=resident),
        ],
        out_specs=pl.BlockSpec((tm, d), lambda i: (i, 0)),
        out_shape=jax.ShapeDtypeStruct((m, d), F32),
        scratch_shapes=[pltpu.VMEM((tm, d), BF16)],
        compiler_params=_params("parallel"),
        name="cross",
    )(x, g.reshape(1, d), wq, kv, wo)


def _t5_bucket(dist):
    n = np.maximum(dist, 0)
    max_exact = NUM_BUCKETS // 2
    nf = np.maximum(n, max_exact).astype(np.float32)
    large = max_exact + (np.log(nf / np.float32(max_exact)) / np.float32(math.log(MAX_DISTANCE / max_exact))
                         * np.float32(NUM_BUCKETS - max_exact)).astype(np.int32)
    large = np.minimum(large, NUM_BUCKETS - 1)
    return np.where(n < max_exact, n, large).astype(np.int32)


def _bias_by_distance(bias_hb, dists):
    onehot = _t5_bucket(dists)[None, :] == np.arange(NUM_BUCKETS)[:, None]
    return jnp.sum(jnp.where(onehot[None], bias_hb[:, :, None], 0.0), axis=1)


def _toeplitz(g, rows, cols):
    heads, n = g.shape
    assert n == rows + cols - 1
    width = rows + cols
    tiled = jnp.broadcast_to(jnp.pad(g, ((0, 0), (0, 1)))[:, None, :], (heads, rows, width))
    skew = tiled.reshape(heads, rows * width)[:, :rows * (width - 1)].reshape(heads, rows, width - 1)
    return skew[:, :, rows - 1:rows - 1 + cols]


def _moba_bias_tables(bias_hb):
    blk = MOBA_BLOCK
    d_own = np.arange(2 * blk - 1) - (blk - 1)
    g_own = jnp.where(d_own >= 0, _bias_by_distance(bias_hb, np.maximum(d_own, 0)), MASKED)
    g_prev = _bias_by_distance(bias_hb, d_own + blk)
    far = int(_t5_bucket(np.array(blk + 1)))
    bfar = jnp.broadcast_to(bias_hb[:, far][:, None, None], (bias_hb.shape[0], 1, blk))
    return _toeplitz(g_own, blk, blk), _toeplitz(g_prev, blk, blk), bfar


def _swa_bias_table(bias_hb):
    blk = SWA_BLOCK
    dist = np.arange(3 * blk - 1) - (blk - 1)
    ok = (dist >= 0) & (dist < SWA_WINDOW)
    g = jnp.where(ok, _bias_by_distance(bias_hb, np.maximum(dist, 0)), MASKED)
    t = _toeplitz(g, 2 * blk, blk)
    t = t.reshape(N_SWA_KV_HEADS, SWA_GROUP, 2 * blk, blk).transpose(0, 2, 1, 3)
    return t.reshape(N_SWA_KV_HEADS, 2 * blk, SWA_GROUP * blk)


def kernel(x, mem, rel_bias, g_final, g_ffn1, w1_gate, w1_up, w1_down, g_mix, w_in, b_in, sinks, g_out_moba, g_out_swa, w_out, g_cross, g_mem, w_cq, w_ck, w_cv, w_co, g_ffn2, w2_gate, w2_up, w2_down):
    b, s, d = x.shape
    n_mem = mem.shape[1]
    depth = w_in.shape[0]
    assert s % MOBA_BLOCK == 0 and s % SWA_BLOCK == 0 and d % LANES == 0

    bias_moba = rel_bias[:, :N_MOBA_HEADS].T.astype(F32)
    bias_swa = rel_bias[:, N_MOBA_HEADS:].T.astype(F32)
    town, tprev, bfar = _moba_bias_tables(bias_moba * LOG2E)
    tswa = _swa_bias_table(bias_swa * LOG2E)

    xt = x.reshape(b * s, d)
    memt = mem.reshape(b * n_mem, d)
    for l in range(depth):
        w1 = [w.astype(BF16) for w in (w1_gate[l], w1_up[l], w1_down[l])]
        w2 = (w2_gate[l], w2_up[l], w2_down[l])
        if _can_side_cast(b * s, d, w1[0].shape[1]) and all(p.shape == q.shape for p, q in zip(w1, w2)):
            xt, *w2 = _ffn(xt, g_ffn1[l], *w1, cast=w2)
        else:
            xt = _ffn(xt, g_ffn1[l], *w1)
            w2 = [w.astype(BF16) for w in w2]

        nrm = _rmsnorm(xt, g_mix[l], BF16)
        proj = _proj_slabs(nrm, w_in[l].astype(BF16), b_in[l]).reshape(N_SLABS, b, s, HEAD_DIM)
        o_m = _moba(proj, town, tprev, bfar).reshape(N_MOBA_HEADS, b * s, HEAD_DIM)
        sink = jnp.broadcast_to((sinks[l].astype(F32) * LOG2E).reshape(N_SWA_KV_HEADS, SWA_GROUP, 1),
                                (N_SWA_KV_HEADS, SWA_GROUP, SWA_BLOCK)).reshape(N_SWA_KV_HEADS, 1, SWA_GROUP * SWA_BLOCK)
        o_s = _swa(proj, tswa, sink).reshape(N_SWA_HEADS, b * s, HEAD_DIM)
        o = _head_norm(o_m, o_s, g_out_moba[l], g_out_swa[l])
        xt = _matmul_residual(o, w_out[l].astype(BF16), xt)

        mem_n = _rmsnorm(memt, g_mem[l], BF16)
        w_kv = jnp.concatenate([w_ck[l], w_cv[l]], axis=1).astype(BF16)
        kv = _matmul(mem_n, w_kv, BF16).reshape(b, n_mem, 2 * CROSS_W)
        xt = _cross(xt, g_cross[l], w_cq[l].astype(BF16), kv, w_co[l].astype(BF16), s)

        xt = _ffn(xt, g_ffn2[l], *w2, g_out=g_final if l == depth - 1 else None)
    return xt.reshape(b, s, d)
```

```python
import functools
import math

import numpy as np
import jax
import jax.numpy as jnp
from jax import lax
from jax.experimental import pallas as pl
from jax.experimental.pallas import tpu as pltpu

HEAD_DIM = 128
N_MOBA_HEADS = 16
N_SWA_HEADS = 16
N_SWA_KV_HEADS = 4
SWA_GROUP = N_SWA_HEADS // N_SWA_KV_HEADS
MOBA_BLOCK = 256
MOBA_TOPK = 3
SWA_WINDOW = 128
SWA_BLOCK = 128
N_CROSS_HEADS = 4
NUM_BUCKETS = 32
MAX_DISTANCE = 128
EPS = 1e-6

MOBA_W = N_MOBA_HEADS * HEAD_DIM
SWA_W = N_SWA_HEADS * HEAD_DIM
SWA_KV_W = N_SWA_KV_HEADS * HEAD_DIM
CROSS_W = N_CROSS_HEADS * HEAD_DIM

SLAB_QM = 0
SLAB_KM = SLAB_QM + N_MOBA_HEADS
SLAB_VM = SLAB_KM + N_MOBA_HEADS
SLAB_QS = SLAB_VM + N_MOBA_HEADS
SLAB_KS = SLAB_QS + N_SWA_HEADS
SLAB_VS = SLAB_KS + N_SWA_KV_HEADS
N_SLABS = SLAB_VS + N_SWA_KV_HEADS

MASKED = -1e30
LOG2E = math.log2(math.e)
SCORE_SCALE = HEAD_DIM ** -0.5 * LOG2E
LANES = 128
V7X_VMEM_LIMIT_BYTES = 56 * 1024 * 1024

F32 = jnp.float32
BF16 = jnp.bfloat16

_NT = (((1,), (1,)), ((), ()))
_TN = (((0,), (0,)), ((), ()))


def _tile(n, want):
    if n <= want:
        return n
    t = want
    while n % t:
        t -= 8
    return t


def _params(*semantics):
    return pltpu.CompilerParams(dimension_semantics=semantics, vmem_limit_bytes=V7X_VMEM_LIMIT_BYTES)


def _rms_rows(x, g):
    ms = jnp.mean(x * x, axis=-1, keepdims=True)
    return x * lax.rsqrt(ms + EPS) * g


def _rmsnorm_body(x_ref, g_ref, o_ref):
    o_ref[...] = _rms_rows(x_ref[...], g_ref[...]).astype(o_ref.dtype)


def _rmsnorm(x, g, out_dtype):
    m, d = x.shape
    tm = _tile(m, 256)
    return pl.pallas_call(
        _rmsnorm_body,
        grid=(m // tm,),
        in_specs=[pl.BlockSpec((tm, d), lambda i: (i, 0)), pl.BlockSpec((1, d), lambda i: (0, 0))],
        out_specs=pl.BlockSpec((tm, d), lambda i: (i, 0)),
        out_shape=jax.ShapeDtypeStruct((m, d), out_dtype),
        compiler_params=_params("parallel"),
        name="rmsnorm",
    )(x, g.reshape(1, d))


FFN_ROW_CHUNK = 64


def _ffn_body(*refs, n_chunks, final_norm, side_cast):
    x_hbm, g_ref, gout_ref, wg_hbm, wu_hbm, wd_hbm = refs[:6]
    refs = refs[6:]
    if side_cast:
        src_hbm, refs = refs[:3], refs[3:]
    o_hbm, refs = refs[0], refs[1:]
    if side_cast:
        dst_hbm, refs = refs[:3], refs[3:]
    tile_buf, h_ref, wg_buf, wu_buf, wd_buf, sem, x_sem, o_sem = refs[:8]
    if side_cast:
        cin, cout, cin_sem, cout_sem = refs[8:11], refs[11:14], refs[14], refs[15]
    i = pl.program_id(0)
    n_tiles = pl.num_programs(0)
    tm = tile_buf.shape[1]
    tf = wd_buf.shape[1]
    last = n_chunks - 1
    backwards = lax.rem(i, 2) == 1
    cur = lax.rem(i, 2)
    o_ref = tile_buf.at[cur]

    def tile_rows(t):
        return pl.ds(pl.multiple_of(t * tm, tm), tm)

    def x_in(t, slot):
        return pltpu.make_async_copy(x_hbm.at[tile_rows(t), :], tile_buf.at[slot], x_sem.at[slot])

    def o_out(t, slot):
        return pltpu.make_async_copy(tile_buf.at[slot], o_hbm.at[tile_rows(t), :], o_sem.at[slot])

    @pl.when(i == 0)
    def _():
        x_in(0, 0).start()

    def fetch(pos, slot):
        cols = pl.ds(pl.multiple_of(jnp.where(backwards, last - pos, pos) * tf, tf), tf)
        return (pltpu.make_async_copy(wg_hbm.at[:, cols], wg_buf.at[slot], sem.at[0, slot]),
                pltpu.make_async_copy(wu_hbm.at[:, cols], wu_buf.at[slot], sem.at[1, slot]),
                pltpu.make_async_copy(wd_hbm.at[cols, :], wd_buf.at[slot], sem.at[2, slot]))

    def start(pos, slot):
        for copy in fetch(pos, slot):
            copy.start()

    def wait(pos, slot):
        for copy in fetch(pos, slot):
            copy.wait()

    def pieces(hbm, pos):
        span = cin[0].shape[1]
        own = pl.ds(pl.multiple_of(i * span, span), span)
        chunk = pl.ds(pl.multiple_of(pos * tf, tf), tf)
        return hbm[0].at[own, chunk], hbm[1].at[own, chunk], hbm[2].at[chunk, own]

    def cast_in(pos, slot):
        return [pltpu.make_async_copy(src, cin[k].at[slot], cin_sem.at[k, slot])
                for k, src in enumerate(pieces(src_hbm, pos))]

    def cast_out(pos, slot):
        return [pltpu.make_async_copy(cout[k].at[slot], dst, cout_sem.at[k, slot])
                for k, dst in enumerate(pieces(dst_hbm, pos))]

    def side(copies, method, when=None):
        if not side_cast:
            return

        def go():
            for copy in copies():
                getattr(copy, method)()

        if when is None:
            go()
        else:
            pl.when(when)(go)

    @pl.when(i == 0)
    def _():
        start(0, 0)

    side(lambda: cast_in(0, 0), "start")
    x_in(i, cur).wait()

    def norm_rows(r, carry):
        rows = pl.ds(pl.multiple_of(r * FFN_ROW_CHUNK, FFN_ROW_CHUNK), FFN_ROW_CHUNK)
        h_ref[rows, :] = _rms_rows(o_ref[rows, :], g_ref[...]).astype(BF16)
        return carry

    lax.fori_loop(0, tm // FFN_ROW_CHUNK, norm_rows, 0)

    @pl.when(i == 0)
    def _():
        wait(0, 0)

    def compute(slot):
        if side_cast:
            for k in range(3):
                cout[k][slot] = cin[k][slot].astype(BF16)
        h = h_ref[...]
        gate = jnp.dot(h, wg_buf[slot], preferred_element_type=F32)
        up = jnp.dot(h, wu_buf[slot], preferred_element_type=F32)
        act = (0.5 * gate * jax.nn.sigmoid(gate) * up).astype(BF16)
        o_ref[...] += jnp.dot(act, wd_buf[slot], preferred_element_type=F32)

    def chunk_pair(pp, carry):
        pos = 2 * pp

        @pl.when((pp == 1) & (i >= 1))
        def _():
            o_out(i - 1, 1 - cur).wait()

        @pl.when((pp == 1) & (i + 1 < n_tiles))
        def _():
            x_in(i + 1, 1 - cur).start()

        side(lambda: cast_in(pos, 0), "wait")
        side(lambda: cast_out(pos - 1, 1), "start", pos >= 1)
        side(lambda: cast_out(pos - 2, 0), "wait", pos >= 2)

        @pl.when(pos < last)
        def _():
            start(pos + 1, 1)
            side(lambda: cast_in(pos + 1, 1), "start")

        compute(0)

        @pl.when(pos < last)
        def _():
            wait(pos + 1, 1)
            side(lambda: cast_in(pos + 1, 1), "wait")
            side(lambda: cast_out(pos, 0), "start")
            side(lambda: cast_out(pos - 1, 1), "wait", pos >= 1)
            start(pos + 2, 0)
            side(lambda: cast_in(pos + 2, 0), "start")
            compute(1)
            wait(pos + 2, 0)

        return carry

    lax.fori_loop(0, (n_chunks + 1) // 2, chunk_pair, 0)
    side(lambda: cast_out(last, 0), "start")
    side(lambda: cast_out(last - 1, 1), "wait")
    side(lambda: cast_out(last, 0), "wait")

    if final_norm:
        def out_rows(r, carry):
            rows = pl.ds(pl.multiple_of(r * FFN_ROW_CHUNK, FFN_ROW_CHUNK), FFN_ROW_CHUNK)
            o_ref[rows, :] = _rms_rows(o_ref[rows, :], gout_ref[...])
            return carry

        lax.fori_loop(0, tm // FFN_ROW_CHUNK, out_rows, 0)

    o_out(i, cur).start()

    @pl.when(i == n_tiles - 1)
    def _():
        o_out(i, cur).wait()


def _ffn_tiles(m, d, f):
    tm = _tile(m, 1024)
    tf = 256 if f % 256 == 0 else LANES
    return tm, tf


def _can_side_cast(m, d, f):
    tm, tf = _ffn_tiles(m, d, f)
    tiles = m // tm
    return d % tiles == 0 and (d // tiles) % LANES == 0


def _ffn(x, g, wg, wu, wd, g_out=None, cast=None):
    m, d = x.shape
    f = wg.shape[1]
    tm, tf = _ffn_tiles(m, d, f)
    n_chunks = f // tf
    tiles = m // tm
    assert f % tf == 0 and n_chunks % 2 == 1 and n_chunks >= 3 and tm % FFN_ROW_CHUNK == 0
    final_norm = g_out is not None
    side_cast = cast is not None
    any_spec = pl.BlockSpec(memory_space=pl.ANY)
    in_specs = [
        any_spec,
        pl.BlockSpec((1, d), lambda i: (0, 0)),
        pl.BlockSpec((1, d), lambda i: (0, 0)),
        any_spec, any_spec, any_spec,
    ]
    out_specs = [any_spec]
    out_shape = [jax.ShapeDtypeStruct((m, d), F32)]
    scratch = [
        pltpu.VMEM((2, tm, d), F32),
        pltpu.VMEM((tm, d), BF16),
        pltpu.VMEM((2, d, tf), BF16),
        pltpu.VMEM((2, d, tf), BF16),
        pltpu.VMEM((2, tf, d), BF16),
        pltpu.SemaphoreType.DMA((3, 2)),
        pltpu.SemaphoreType.DMA((2,)),
        pltpu.SemaphoreType.DMA((2,)),
    ]
    operands = [x, g.reshape(1, d), (g_out if final_norm else g).reshape(1, d), wg, wu, wd]
    if side_cast:
        assert _can_side_cast(m, d, f) and all(c.shape == w.shape for c, w in zip(cast, (wg, wu, wd)))
        span = d // tiles
        piece_shapes = [(span, tf), (span, tf), (tf, span)]
        in_specs += [any_spec] * 3
        out_specs += [any_spec] * 3
        out_shape += [jax.ShapeDtypeStruct(c.shape, BF16) for c in cast]
        scratch += [pltpu.VMEM((2,) + s, F32) for s in piece_shapes]
        scratch += [pltpu.VMEM((2,) + s, BF16) for s in piece_shapes]
        scratch += [pltpu.SemaphoreType.DMA((3, 2)), pltpu.SemaphoreType.DMA((3, 2))]
        operands += list(cast)
    body = functools.partial(_ffn_body, n_chunks=n_chunks, final_norm=final_norm, side_cast=side_cast)
    out = pl.pallas_call(
        body,
        grid=(tiles,),
        in_specs=in_specs,
        out_specs=out_specs,
        out_shape=out_shape,
        scratch_shapes=scratch,
        compiler_params=_params("arbitrary"),
        name="ffn",
    )(*operands)
    return out if side_cast else out[0]


def _in_proj_body(x_hbm, g_ref, w_ref, b_ref, o_ref, x_buf, a_ref, sem):
    i = pl.program_id(0)
    tm = x_buf.shape[0]

    def x_in(t):
        return pltpu.make_async_copy(x_hbm.at[pl.ds(pl.multiple_of(t * tm, tm), tm), :], x_buf, sem.at[0])

    @pl.when(pl.program_id(1) == 0)
    def _():
        @pl.when(i == 0)
        def _():
            x_in(0).start()

        x_in(i).wait()

        def norm_rows(r, carry):
            rows = pl.ds(pl.multiple_of(r * FFN_ROW_CHUNK, FFN_ROW_CHUNK), FFN_ROW_CHUNK)
            a_ref[rows, :] = _rms_rows(x_buf[rows, :], g_ref[...]).astype(BF16)
            return carry

        lax.fori_loop(0, tm // FFN_ROW_CHUNK, norm_rows, 0)

        @pl.when(i + 1 < pl.num_programs(0))
        def _():
            x_in(i + 1).start()

    acc = jnp.dot(a_ref[...], w_ref[...], preferred_element_type=F32) + b_ref[...]
    for c in range(o_ref.shape[0]):
        o_ref[c] = acc[:, c * LANES:(c + 1) * LANES].astype(o_ref.dtype)


def _in_proj(x, g, w, b):
    m, k = x.shape
    n = w.shape[1]
    tm = _tile(m, 1024)
    tn = _tile(n, 1024)
    assert tm % FFN_ROW_CHUNK == 0
    return pl.pallas_call(
        _in_proj_body,
        grid=(m // tm, n // tn),
        in_specs=[
            pl.BlockSpec(memory_space=pl.ANY),
            pl.BlockSpec((1, k), lambda i, j: (0, 0)),
            pl.BlockSpec((k, tn), lambda i, j: (0, j)),
            pl.BlockSpec((1, tn), lambda i, j: (0, j)),
        ],
        out_specs=pl.BlockSpec((tn // LANES, tm, LANES), lambda i, j: (j, i, 0)),
        out_shape=jax.ShapeDtypeStruct((n // LANES, m, LANES), BF16),
        scratch_shapes=[pltpu.VMEM((tm, k), F32), pltpu.VMEM((tm, k), BF16), pltpu.SemaphoreType.DMA((1,))],
        compiler_params=_params("arbitrary", "arbitrary"),
        name="in_proj",
    )(x, g.reshape(1, k), w, b.reshape(1, n))


def _matmul_body(a_ref, w_ref, o_ref):
    o_ref[...] = jnp.dot(a_ref[...], w_ref[...], preferred_element_type=F32).astype(o_ref.dtype)


def _matmul(a, w, out_dtype):
    m, k = a.shape
    n = w.shape[1]
    tm = _tile(m, 1024)
    tn = _tile(n, 512)
    return pl.pallas_call(
        _matmul_body,
        grid=(m // tm, n // tn),
        in_specs=[pl.BlockSpec((tm, k), lambda i, j: (i, 0)), pl.BlockSpec((k, tn), lambda i, j: (0, j))],
        out_specs=pl.BlockSpec((tm, tn), lambda i, j: (i, j)),
        out_shape=jax.ShapeDtypeStruct((m, n), out_dtype),
        compiler_params=_params("parallel", "parallel"),
        name="matmul",
    )(a, w)


def _matmul_residual_body(a_ref, w_ref, r_ref, o_ref):
    o_ref[...] = r_ref[...] + jnp.dot(a_ref[...], w_ref[...], preferred_element_type=F32)


def _matmul_residual(a, w, res):
    m, k = a.shape
    n = w.shape[1]
    tm = _tile(m, 1024)
    tn = _tile(n, 1024)
    return pl.pallas_call(
        _matmul_residual_body,
        grid=(m // tm, n // tn),
        in_specs=[
            pl.BlockSpec((tm, k), lambda i, j: (i, 0)),
            pl.BlockSpec((k, tn), lambda i, j: (0, j)),
            pl.BlockSpec((tm, tn), lambda i, j: (i, j)),
        ],
        out_specs=pl.BlockSpec((tm, tn), lambda i, j: (i, j)),
        out_shape=jax.ShapeDtypeStruct((m, n), F32),
        compiler_params=_params("parallel", "parallel"),
        name="out_proj",
    )(a, w, res)


def _fold_rows(x, op, rows=8):
    while x.shape[0] > rows:
        half = x.shape[0] // 2
        x = op(x[:half], x[half:])
    return x


def _moba_body(q_ref, k_ref, v_ref, town_ref, tprev_ref, bfar_ref, o_ref, km_ref, sel_ref, s_ref, l_ref, acc_ref,
               *, n_blocks, topk, heads):
    n = pl.program_id(2)
    blk = MOBA_BLOCK
    pair = 2 * blk
    scale = SCORE_SCALE
    group = range(heads)

    @pl.when(n == 0)
    def _():
        for g in group:
            for j in range(n_blocks):
                kj = k_ref[g, 0, j * blk:(j + 1) * blk, :].astype(F32)
                km_ref[g, j:j + 1, :] = jnp.mean(kj, axis=0, keepdims=True)

    q = [q_ref[g, 0] for g in group]

    for g in group:
        gate = lax.dot_general(km_ref[g].astype(BF16), q[g], _NT, preferred_element_type=F32)
        rows = lax.broadcasted_iota(jnp.int32, gate.shape, 0)
        gate = jnp.where(rows < n, gate, -jnp.inf)
        sel = jnp.zeros(gate.shape, jnp.bool_)
        for _ in range(topk):
            best = jnp.max(gate, axis=0, keepdims=True)
            first = jnp.min(jnp.where(gate == best, rows, n_blocks), axis=0, keepdims=True)
            pick = (rows == first) & (best > -jnp.inf)
            sel = sel | pick
            gate = jnp.where(pick, -jnp.inf, gate)
        sel_ref[g] = jnp.where(sel, 0.0, MASKED)

    def scores(g, rows):
        return lax.dot_general(k_ref[g, 0, rows, :], q[g], _NT, preferred_element_type=F32) * scale

    def pv(g, rows, p):
        return lax.dot_general(v_ref[g, 0, rows, :], p.astype(BF16), _TN, preferred_element_type=F32)

    jp = jnp.maximum(n - 1, 0)
    own_rows = pl.ds(pl.multiple_of(n * blk, blk), blk)
    prev_rows = pl.ds(pl.multiple_of(jp * blk, blk), blk)
    m8 = []
    for g in group:
        s_own = scores(g, own_rows) + town_ref[g]
        s_prev = scores(g, prev_rows) + tprev_ref[g] + sel_ref[g, pl.ds(jp, 1), :]
        s_ref[g, 0:blk, :] = s_own
        s_ref[g, blk:pair, :] = s_prev
        m8.append(jnp.maximum(_fold_rows(s_own, jnp.maximum), _fold_rows(s_prev, jnp.maximum)))

    n_far = n - 1
    n_pairs = lax.shift_right_logical(n, 1)

    def far_scores(c, m8):
        key_rows = pl.ds(pl.multiple_of(c * pair, pair), pair)
        base = pl.multiple_of(pair + c * pair, pair)
        j1 = 2 * c + 1
        out = []
        for g in group:
            bfar = bfar_ref[g]
            s0 = scores(g, pl.ds(pl.multiple_of(c * pair, blk), blk)) + (sel_ref[g, pl.ds(2 * c, 1), :] + bfar)
            s1 = scores(g, pl.ds(pl.multiple_of(c * pair + blk, blk), blk)) + jnp.where(
                j1 < n_far, sel_ref[g, pl.ds(j1, 1), :] + bfar, MASKED)
            s_ref[g, pl.ds(base, blk), :] = s0
            s_ref[g, pl.ds(base + blk, blk), :] = s1
            out.append(jnp.maximum(m8[g], jnp.maximum(_fold_rows(s0, jnp.maximum), _fold_rows(s1, jnp.maximum))))
        return tuple(out)

    m8 = lax.fori_loop(0, n_pairs, far_scores, tuple(m8))
    m = [jnp.max(m8[g], axis=0, keepdims=True) for g in group]

    for g in group:
        p_own = jnp.exp2(s_ref[g, 0:blk, :] - m[g])
        p_prev = jnp.exp2(s_ref[g, blk:pair, :] - m[g])
        l_ref[g] = _fold_rows(p_own, jnp.add) + _fold_rows(p_prev, jnp.add)
        acc_ref[g] = pv(g, own_rows, p_own) + pv(g, prev_rows, p_prev)

    def far_pv(c, carry):
        key_rows = pl.ds(pl.multiple_of(c * pair, pair), pair)
        base = pl.multiple_of(pair + c * pair, pair)
        for g in group:
            part = 0.0
            for half in range(2):
                rows = pl.ds(pl.multiple_of(c * pair + half * blk, blk), blk)
                p = jnp.exp2(s_ref[g, pl.ds(base + half * blk, blk), :] - m[g])
                l_ref[g] += _fold_rows(p, jnp.add)
                part = part + pv(g, rows, p)
            acc_ref[g] += part
        return carry

    lax.fori_loop(0, n_pairs, far_pv, 0)
    for g in group:
        l = jnp.sum(l_ref[g], axis=0, keepdims=True)
        o_ref[g, 0] = (acc_ref[g] / l).T.astype(o_ref.dtype)


def _moba(proj, town, tprev, bfar):
    _, b, s, _ = proj.shape
    blk = MOBA_BLOCK
    n_blocks = s // blk
    heads = 4
    body = functools.partial(_moba_body, n_blocks=n_blocks, topk=min(MOBA_TOPK, n_blocks), heads=heads)
    return pl.pallas_call(
        body,
        grid=(b, N_MOBA_HEADS // heads, n_blocks),
        in_specs=[
            pl.BlockSpec((heads, 1, blk, HEAD_DIM), lambda bi, h, n: (SLAB_QM // heads + h, bi, n, 0)),
            pl.BlockSpec((heads, 1, s, HEAD_DIM), lambda bi, h, n: (SLAB_KM // heads + h, bi, 0, 0)),
            pl.BlockSpec((heads, 1, s, HEAD_DIM), lambda bi, h, n: (SLAB_VM // heads + h, bi, 0, 0)),
            pl.BlockSpec((heads, blk, blk), lambda bi, h, n: (h, 0, 0)),
            pl.BlockSpec((heads, blk, blk), lambda bi, h, n: (h, 0, 0)),
            pl.BlockSpec((heads, 1, blk), lambda bi, h, n: (h, 0, 0)),
        ],
        out_specs=pl.BlockSpec((heads, 1, blk, HEAD_DIM), lambda bi, h, n: (h, bi, n, 0)),
        out_shape=jax.ShapeDtypeStruct((N_MOBA_HEADS, b, s, HEAD_DIM), F32),
        scratch_shapes=[
            pltpu.VMEM((heads, n_blocks, HEAD_DIM), F32),
            pltpu.VMEM((heads, n_blocks, blk), F32),
            pltpu.VMEM((heads, 2 * blk * (1 + (n_blocks - 1) // 2), blk), F32),
            pltpu.VMEM((heads, 8, blk), F32),
            pltpu.VMEM((heads, HEAD_DIM, blk), F32),
        ],
        compiler_params=_params("parallel", "parallel", "arbitrary"),
        name="moba",
    )(proj, proj, proj, town, tprev, bfar)


def _swa_body(q_ref, k_ref, v_ref, bias_ref, sink_ref, o_ref, *, blocks_per_step):
    blk = SWA_BLOCK
    scale = SCORE_SCALE
    sink = sink_ref[0]
    blocks = range(blocks_per_step)

    def band(ref, r):
        n = pl.program_id(2) * blocks_per_step + r
        prev_rows = pl.ds(pl.multiple_of(jnp.maximum(n - 1, 0) * blk, blk), blk)
        own_rows = pl.ds(pl.multiple_of(n * blk, blk), blk)
        return jnp.concatenate([ref[0, 0, prev_rows, :], ref[0, 0, own_rows, :]], axis=0)

    raw = []
    for r in blocks:
        q = jnp.concatenate([q_ref[g, 0, r * blk:(r + 1) * blk, :] for g in range(SWA_GROUP)], axis=0)
        raw.append(lax.dot_general(band(k_ref, r), q, _NT, preferred_element_type=F32))
    soft = []
    for r in blocks:
        n = pl.program_id(2) * blocks_per_step + r
        s = raw[r] * scale + bias_ref[0]
        key_row = lax.broadcasted_iota(jnp.int32, s.shape, 0)
        s = jnp.where((key_row < blk) & (n == 0), MASKED, s)
        m = jnp.maximum(jnp.max(s, axis=0, keepdims=True), sink)
        p = jnp.exp2(s - m)
        soft.append((p.astype(BF16), jnp.sum(p, axis=0, keepdims=True) + jnp.exp2(sink - m)))
    for r in blocks:
        p, l = soft[r]
        o = lax.dot_general(band(v_ref, r), p, _TN, preferred_element_type=F32) / l
        for g in range(SWA_GROUP):
            o_ref[g, 0, r * blk:(r + 1) * blk, :] = o[:, g * blk:(g + 1) * blk].T.astype(o_ref.dtype)


def _swa(proj, bias, sink):
    _, b, s, _ = proj.shape
    blk = SWA_BLOCK
    gq = SWA_GROUP * blk
    per_step = 8 if (s // blk) % 8 == 0 else 1
    rows = per_step * blk
    return pl.pallas_call(
        functools.partial(_swa_body, blocks_per_step=per_step),
        grid=(b, N_SWA_KV_HEADS, s // rows),
        in_specs=[
            pl.BlockSpec((SWA_GROUP, 1, rows, HEAD_DIM), lambda bi, h, n: (SLAB_QS // SWA_GROUP + h, bi, n, 0)),
            pl.BlockSpec((1, 1, s, HEAD_DIM), lambda bi, h, n: (SLAB_KS + h, bi, 0, 0)),
            pl.BlockSpec((1, 1, s, HEAD_DIM), lambda bi, h, n: (SLAB_VS + h, bi, 0, 0)),
            pl.BlockSpec((1, 2 * blk, gq), lambda bi, h, n: (h, 0, 0)),
            pl.BlockSpec((1, 1, gq), lambda bi, h, n: (h, 0, 0)),
        ],
        out_specs=pl.BlockSpec((SWA_GROUP, 1, rows, HEAD_DIM), lambda bi, h, n: (h, bi, n, 0)),
        out_shape=jax.ShapeDtypeStruct((N_SWA_HEADS, b, s, HEAD_DIM), F32),
        compiler_params=_params("parallel", "parallel", "arbitrary"),
        name="swa",
    )(proj, proj, proj, bias, sink)


def _head_norm_body(om_ref, os_ref, gm_ref, gs_ref, o_ref):
    col = 0
    for x_ref, g_ref in ((om_ref, gm_ref), (os_ref, gs_ref)):
        heads = x_ref.shape[0]
        ss = sum(jnp.sum(jnp.square(x_ref[c]), axis=-1, keepdims=True) for c in range(heads))
        r = lax.rsqrt(ss / (heads * HEAD_DIM) + EPS)
        for c in range(heads):
            o_ref[:, col:col + HEAD_DIM] = (x_ref[c] * r * g_ref[:, c * HEAD_DIM:(c + 1) * HEAD_DIM]).astype(o_ref.dtype)
            col += HEAD_DIM


def _head_norm(om, osw, gm, gs):
    hm, m, _ = om.shape
    hs = osw.shape[0]
    tm = _tile(m, 256)
    width = (hm + hs) * HEAD_DIM
    return pl.pallas_call(
        _head_norm_body,
        grid=(m // tm,),
        in_specs=[
            pl.BlockSpec((hm, tm, HEAD_DIM), lambda i: (0, i, 0)),
            pl.BlockSpec((hs, tm, HEAD_DIM), lambda i: (0, i, 0)),
            pl.BlockSpec((1, hm * HEAD_DIM), lambda i: (0, 0)),
            pl.BlockSpec((1, hs * HEAD_DIM), lambda i: (0, 0)),
        ],
        out_specs=pl.BlockSpec((tm, width), lambda i: (i, 0)),
        out_shape=jax.ShapeDtypeStruct((m, width), BF16),
        compiler_params=_params("parallel"),
        name="head_norm",
    )(om, osw, gm.reshape(1, -1), gs.reshape(1, -1))


def _cross_body(x_ref, g_ref, wq_ref, kv_ref, wo_ref, o_ref, h_ref):
    scale = SCORE_SCALE
    tm = x_ref.shape[0]

    def norm_rows(r, carry):
        rows = pl.ds(pl.multiple_of(r * FFN_ROW_CHUNK, FFN_ROW_CHUNK), FFN_ROW_CHUNK)
        h_ref[rows, :] = _rms_rows(x_ref[rows, :], g_ref[...]).astype(BF16)
        return carry

    lax.fori_loop(0, tm // FFN_ROW_CHUNK, norm_rows, 0)
    q = jnp.dot(h_ref[...], wq_ref[...], preferred_element_type=F32).astype(BF16)
    outs = []
    for hd in range(N_CROSS_HEADS):
        cols = slice(hd * HEAD_DIM, (hd + 1) * HEAD_DIM)
        k = kv_ref[0, :, cols]
        v = kv_ref[0, :, CROSS_W + hd * HEAD_DIM:CROSS_W + (hd + 1) * HEAD_DIM]
        s = lax.dot_general(q[:, cols], k, _NT, preferred_element_type=F32) * scale
        p = jnp.exp2(s - jnp.max(s, axis=-1, keepdims=True))
        l = jnp.sum(p, axis=-1, keepdims=True)
        outs.append((jnp.dot(p.astype(BF16), v, preferred_element_type=F32) / l).astype(BF16))
    o = jnp.concatenate(outs, axis=-1)
    o_ref[...] = x_ref[...] + jnp.dot(o, wo_ref[...], preferred_element_type=F32)


def _cross(x, g, wq, kv, wo, seq):
    m, d = x.shape
    n_mem = kv.shape[1]
    tm = _tile(seq, 512)
    assert tm % FFN_ROW_CHUNK == 0
    tiles_per_seq = seq // tm
    resident = pl.Buffered(1)
    return pl.pallas_call(
        _cross_body,
        grid=(m // tm,),
        in_specs=[
            pl.BlockSpec((tm, d), lambda i: (i, 0)),
            pl.BlockSpec((1, d), lambda i: (0, 0)),
            pl.BlockSpec((d, CROSS_W), lambda i: (0, 0), pipeline_mode=resident),
            pl.BlockSpec((1, n_mem, 2 * CROSS_W), lambda i: (i // tiles_per_seq, 0, 0)),
            pl.BlockSpec((CROSS_W, d), lambda i: (0, 0), pipeline_mode=resident),
        ],
        out_specs=pl.BlockSpec((tm, d), lambda i: (i, 0)),
        out_shape=jax.ShapeDtypeStruct((m, d), F32),
        scratch_shapes=[pltpu.VMEM((tm, d), BF16)],
        compiler_params=_params("parallel"),
        name="cross",
    )(x, g.reshape(1, d), wq, kv, wo)


def _t5_bucket(dist):
    n = np.maximum(dist, 0)
    max_exact = NUM_BUCKETS // 2
    nf = np.maximum(n, max_exact).astype(np.float32)
    large = max_exact + (np.log(nf / np.float32(max_exact)) / np.float32(math.log(MAX_DISTANCE / max_exact))
                         * np.float32(NUM_BUCKETS - max_exact)).astype(np.int32)
    large = np.minimum(large, NUM_BUCKETS - 1)
    return np.where(n < max_exact, n, large).astype(np.int32)


def _bias_by_distance(bias_hb, dists):
    onehot = _t5_bucket(dists)[None, :] == np.arange(NUM_BUCKETS)[:, None]
    return jnp.sum(jnp.where(onehot[None], bias_hb[:, :, None], 0.0), axis=1)


def _toeplitz(g, rows, cols):
    heads, n = g.shape
    assert n == rows + cols - 1
    width = rows + cols
    tiled = jnp.broadcast_to(jnp.pad(g, ((0, 0), (0, 1)))[:, None, :], (heads, rows, width))
    skew = tiled.reshape(heads, rows * width)[:, :rows * (width - 1)].reshape(heads, rows, width - 1)
    return skew[:, :, rows - 1:rows - 1 + cols]


def _moba_bias_tables(bias_hb):
    blk = MOBA_BLOCK
    d_own = np.arange(2 * blk - 1) - (blk - 1)
    g_own = jnp.where(d_own >= 0, _bias_by_distance(bias_hb, np.maximum(d_own, 0)), MASKED)
    g_prev = _bias_by_distance(bias_hb, d_own + blk)
    far = int(_t5_bucket(np.array(blk + 1)))
    bfar = jnp.broadcast_to(bias_hb[:, far][:, None, None], (bias_hb.shape[0], 1, blk))
    return _toeplitz(g_own, blk, blk), _toeplitz(g_prev, blk, blk), bfar


def _swa_bias_table(bias_hb):
    blk = SWA_BLOCK
    dist = np.arange(3 * blk - 1) - (blk - 1)
    ok = (dist >= 0) & (dist < SWA_WINDOW)
    g = jnp.where(ok, _bias_by_distance(bias_hb, np.maximum(dist, 0)), MASKED)
    t = _toeplitz(g, 2 * blk, blk)
    t = t.reshape(N_SWA_KV_HEADS, SWA_GROUP, 2 * blk, blk).transpose(0, 2, 1, 3)
    return t.reshape(N_SWA_KV_HEADS, 2 * blk, SWA_GROUP * blk)


def kernel(x, mem, rel_bias, g_final, g_ffn1, w1_gate, w1_up, w1_down, g_mix, w_in, b_in, sinks, g_out_moba, g_out_swa, w_out, g_cross, g_mem, w_cq, w_ck, w_cv, w_co, g_ffn2, w2_gate, w2_up, w2_down):
    b, s, d = x.shape
    n_mem = mem.shape[1]
    depth = w_in.shape[0]
    assert s % MOBA_BLOCK == 0 and s % SWA_BLOCK == 0 and d % LANES == 0

    bias_moba = rel_bias[:, :N_MOBA_HEADS].T.astype(F32)
    bias_swa = rel_bias[:, N_MOBA_HEADS:].T.astype(F32)
    town, tprev, bfar = _moba_bias_tables(bias_moba * LOG2E)
    tswa = _swa_bias_table(bias_swa * LOG2E)

    xt = x.reshape(b * s, d)
    memt = mem.reshape(b * n_mem, d)
    for l in range(depth):
        w1 = [w.astype(BF16) for w in (w1_gate[l], w1_up[l], w1_down[l])]
        w2 = (w2_gate[l], w2_up[l], w2_down[l])
        if _can_side_cast(b * s, d, w1[0].shape[1]) and all(p.shape == q.shape for p, q in zip(w1, w2)):
            xt, *w2 = _ffn(xt, g_ffn1[l], *w1, cast=w2)
        else:
            xt = _ffn(xt, g_ffn1[l], *w1)
            w2 = [w.astype(BF16) for w in w2]

        proj = _in_proj(xt, g_mix[l], w_in[l].astype(BF16), b_in[l]).reshape(N_SLABS, b, s, HEAD_DIM)
        o_m = _moba(proj, town, tprev, bfar).reshape(N_MOBA_HEADS, b * s, HEAD_DIM)
        sink = jnp.broadcast_to((sinks[l].astype(F32) * LOG2E).reshape(N_SWA_KV_HEADS, SWA_GROUP, 1),
                                (N_SWA_KV_HEADS, SWA_GROUP, SWA_BLOCK)).reshape(N_SWA_KV_HEADS, 1, SWA_GROUP * SWA_BLOCK)
        o_s = _swa(proj, tswa, sink).reshape(N_SWA_HEADS, b * s, HEAD_DIM)
        o = _head_norm(o_m, o_s, g_out_moba[l], g_out_swa[l])
        xt = _matmul_residual(o, w_out[l].astype(BF16), xt)

        mem_n = _rmsnorm(memt, g_mem[l], BF16)
        w_kv = jnp.concatenate([w_ck[l], w_cv[l]], axis=1).astype(BF16)
        kv = _matmul(mem_n, w_kv, BF16).reshape(b, n_mem, 2 * CROSS_W)
        xt = _cross(xt, g_cross[l], w_cq[l].astype(BF16), kv, w_co[l].astype(BF16), s)

        xt = _ffn(xt, g_ffn2[l], *w2, g_out=g_final if l == depth - 1 else None)
    return xt.reshape(b, s, d)
```

```python
import functools
import math

import numpy as np
import jax
import jax.numpy as jnp
from jax import lax
from jax.experimental import pallas as pl
from jax.experimental.pallas import tpu as pltpu

HEAD_DIM = 128
N_MOBA_HEADS = 16
N_SWA_HEADS = 16
N_SWA_KV_HEADS = 4
SWA_GROUP = N_SWA_HEADS // N_SWA_KV_HEADS
MOBA_BLOCK = 256
MOBA_TOPK = 3
SWA_WINDOW = 128
SWA_BLOCK = 128
N_CROSS_HEADS = 4
NUM_BUCKETS = 32
MAX_DISTANCE = 128
EPS = 1e-6

CROSS_W = N_CROSS_HEADS * HEAD_DIM

SLAB_QM = 0
SLAB_KM = SLAB_QM + N_MOBA_HEADS
SLAB_VM = SLAB_KM + N_MOBA_HEADS
SLAB_QS = SLAB_VM + N_MOBA_HEADS
SLAB_KS = SLAB_QS + N_SWA_HEADS
SLAB_VS = SLAB_KS + N_SWA_KV_HEADS
N_SLABS = SLAB_VS + N_SWA_KV_HEADS

MASKED = -1e30
LOG2E = math.log2(math.e)
SCORE_SCALE = HEAD_DIM ** -0.5 * LOG2E
LANES = 128
MXU_DIM = 256
V7X_VMEM_LIMIT_BYTES = 56 * 1024 * 1024

FFN_ROWS = 1024
FFN_CHUNK = MXU_DIM
PROJ_ROWS = 1024
PROJ_COLS = 1024
KV_COLS = 512
NORM_ROWS = 256
HEAD_NORM_ROWS = 512
CROSS_ROWS = 512
MOBA_HEADS_PER_STEP = 4
SWA_BLOCKS_PER_STEP = 8

F32 = jnp.float32
BF16 = jnp.bfloat16

_NT = (((1,), (1,)), ((), ()))
_TN = (((0,), (0,)), ((), ()))


def _tile(n, want):
    if n <= want:
        return n
    t = want
    while n % t:
        t -= 8
    return t


def _params(*semantics):
    return pltpu.CompilerParams(dimension_semantics=semantics, vmem_limit_bytes=V7X_VMEM_LIMIT_BYTES)


def _rms_rows(x, g):
    ms = jnp.mean(x * x, axis=-1, keepdims=True)
    return x * lax.rsqrt(ms + EPS) * g


def _rmsnorm_body(x_ref, g_ref, o_ref):
    o_ref[...] = _rms_rows(x_ref[...], g_ref[...]).astype(o_ref.dtype)


def _rmsnorm(x, g, out_dtype):
    m, d = x.shape
    tm = _tile(m, NORM_ROWS)
    return pl.pallas_call(
        _rmsnorm_body,
        grid=(m // tm,),
        in_specs=[pl.BlockSpec((tm, d), lambda i: (i, 0)), pl.BlockSpec((1, d), lambda i: (0, 0))],
        out_specs=pl.BlockSpec((tm, d), lambda i: (i, 0)),
        out_shape=jax.ShapeDtypeStruct((m, d), out_dtype),
        compiler_params=_params("parallel"),
        name="rmsnorm",
    )(x, g.reshape(1, d))


FFN_ROW_CHUNK = 64


def _ffn_body(*refs, n_chunks, final_norm, side_cast):
    x_hbm, g_ref, gout_ref, wg_hbm, wu_hbm, wd_hbm = refs[:6]
    refs = refs[6:]
    if side_cast:
        src_hbm, refs = refs[:3], refs[3:]
    o_hbm, refs = refs[0], refs[1:]
    if side_cast:
        dst_hbm, refs = refs[:3], refs[3:]
    tile_buf, h_ref, wg_buf, wu_buf, wd_buf, sem, x_sem, o_sem = refs[:8]
    if side_cast:
        cin, cout, cin_sem, cout_sem = refs[8:11], refs[11:14], refs[14], refs[15]
    i = pl.program_id(0)
    n_tiles = pl.num_programs(0)
    tm = tile_buf.shape[1]
    tf = wd_buf.shape[1]
    last = n_chunks - 1
    backwards = lax.rem(i, 2) == 1
    cur = lax.rem(i, 2)
    o_ref = tile_buf.at[cur]

    def tile_rows(t):
        return pl.ds(pl.multiple_of(t * tm, tm), tm)

    def x_in(t, slot):
        return pltpu.make_async_copy(x_hbm.at[tile_rows(t), :], tile_buf.at[slot], x_sem.at[slot])

    def o_out(t, slot):
        return pltpu.make_async_copy(tile_buf.at[slot], o_hbm.at[tile_rows(t), :], o_sem.at[slot])

    @pl.when(i == 0)
    def _():
        x_in(0, 0).start()

    def fetch(pos, slot):
        cols = pl.ds(pl.multiple_of(jnp.where(backwards, last - pos, pos) * tf, tf), tf)
        return (pltpu.make_async_copy(wg_hbm.at[:, cols], wg_buf.at[slot], sem.at[0, slot]),
                pltpu.make_async_copy(wu_hbm.at[:, cols], wu_buf.at[slot], sem.at[1, slot]),
                pltpu.make_async_copy(wd_hbm.at[cols, :], wd_buf.at[slot], sem.at[2, slot]))

    def start(pos, slot):
        for copy in fetch(pos, slot):
            copy.start()

    def wait(pos, slot):
        for copy in fetch(pos, slot):
            copy.wait()

    def pieces(hbm, pos):
        span = cin[0].shape[1]
        own = pl.ds(pl.multiple_of(i * span, span), span)
        chunk = pl.ds(pl.multiple_of(pos * tf, tf), tf)
        return hbm[0].at[own, chunk], hbm[1].at[own, chunk], hbm[2].at[chunk, own]

    def cast_in(pos, slot):
        return [pltpu.make_async_copy(src, cin[k].at[slot], cin_sem.at[k, slot])
                for k, src in enumerate(pieces(src_hbm, pos))]

    def cast_out(pos, slot):
        return [pltpu.make_async_copy(cout[k].at[slot], dst, cout_sem.at[k, slot])
                for k, dst in enumerate(pieces(dst_hbm, pos))]

    def side(copies, method, when=None):
        if not side_cast:
            return

        def go():
            for copy in copies():
                getattr(copy, method)()

        if when is None:
            go()
        else:
            pl.when(when)(go)

    @pl.when(i == 0)
    def _():
        start(0, 0)

    side(lambda: cast_in(0, 0), "start")
    x_in(i, cur).wait()

    def norm_rows(r, carry):
        rows = pl.ds(pl.multiple_of(r * FFN_ROW_CHUNK, FFN_ROW_CHUNK), FFN_ROW_CHUNK)
        h_ref[rows, :] = _rms_rows(o_ref[rows, :], g_ref[...]).astype(BF16)
        return carry

    lax.fori_loop(0, tm // FFN_ROW_CHUNK, norm_rows, 0)

    @pl.when(i == 0)
    def _():
        wait(0, 0)

    def compute(slot):
        if side_cast:
            for k in range(3):
                cout[k][slot] = cin[k][slot].astype(BF16)
        h = h_ref[...]
        gate = jnp.dot(h, wg_buf[slot], preferred_element_type=F32)
        up = jnp.dot(h, wu_buf[slot], preferred_element_type=F32)
        act = (0.5 * gate * jax.nn.sigmoid(gate) * up).astype(BF16)
        o_ref[...] += jnp.dot(act, wd_buf[slot], preferred_element_type=F32)

    def chunk_pair(pp, carry):
        pos = 2 * pp

        @pl.when((pp == 1) & (i >= 1))
        def _():
            o_out(i - 1, 1 - cur).wait()

        @pl.when((pp == 1) & (i + 1 < n_tiles))
        def _():
            x_in(i + 1, 1 - cur).start()

        side(lambda: cast_in(pos, 0), "wait")
        side(lambda: cast_out(pos - 1, 1), "start", pos >= 1)
        side(lambda: cast_out(pos - 2, 0), "wait", pos >= 2)

        @pl.when(pos < last)
        def _():
            start(pos + 1, 1)
            side(lambda: cast_in(pos + 1, 1), "start")

        compute(0)

        @pl.when(pos < last)
        def _():
            wait(pos + 1, 1)
            side(lambda: cast_in(pos + 1, 1), "wait")
            side(lambda: cast_out(pos, 0), "start")
            side(lambda: cast_out(pos - 1, 1), "wait", pos >= 1)
            start(pos + 2, 0)
            side(lambda: cast_in(pos + 2, 0), "start")
            compute(1)
            wait(pos + 2, 0)

        return carry

    lax.fori_loop(0, (n_chunks + 1) // 2, chunk_pair, 0)
    side(lambda: cast_out(last, 0), "start")
    side(lambda: cast_out(last - 1, 1), "wait")
    side(lambda: cast_out(last, 0), "wait")

    if final_norm:
        def out_rows(r, carry):
            rows = pl.ds(pl.multiple_of(r * FFN_ROW_CHUNK, FFN_ROW_CHUNK), FFN_ROW_CHUNK)
            o_ref[rows, :] = _rms_rows(o_ref[rows, :], gout_ref[...])
            return carry

        lax.fori_loop(0, tm // FFN_ROW_CHUNK, out_rows, 0)

    o_out(i, cur).start()

    @pl.when(i == n_tiles - 1)
    def _():
        o_out(i, cur).wait()


def _ffn_tiles(m, d, f):
    tm = _tile(m, FFN_ROWS)
    tf = FFN_CHUNK if f % FFN_CHUNK == 0 else LANES
    return tm, tf


def _can_side_cast(m, d, f):
    tm, tf = _ffn_tiles(m, d, f)
    tiles = m // tm
    return d % tiles == 0 and (d // tiles) % LANES == 0


def _ffn(x, g, wg, wu, wd, g_out=None, cast=None):
    m, d = x.shape
    f = wg.shape[1]
    tm, tf = _ffn_tiles(m, d, f)
    n_chunks = f // tf
    tiles = m // tm
    assert f % tf == 0 and n_chunks % 2 == 1 and n_chunks >= 3 and tm % FFN_ROW_CHUNK == 0
    final_norm = g_out is not None
    side_cast = cast is not None
    any_spec = pl.BlockSpec(memory_space=pl.ANY)
    in_specs = [
        any_spec,
        pl.BlockSpec((1, d), lambda i: (0, 0)),
        pl.BlockSpec((1, d), lambda i: (0, 0)),
        any_spec, any_spec, any_spec,
    ]
    out_specs = [any_spec]
    out_shape = [jax.ShapeDtypeStruct((m, d), F32)]
    scratch = [
        pltpu.VMEM((2, tm, d), F32),
        pltpu.VMEM((tm, d), BF16),
        pltpu.VMEM((2, d, tf), BF16),
        pltpu.VMEM((2, d, tf), BF16),
        pltpu.VMEM((2, tf, d), BF16),
        pltpu.SemaphoreType.DMA((3, 2)),
        pltpu.SemaphoreType.DMA((2,)),
        pltpu.SemaphoreType.DMA((2,)),
    ]
    operands = [x, g.reshape(1, d), (g_out if final_norm else g).reshape(1, d), wg, wu, wd]
    if side_cast:
        assert _can_side_cast(m, d, f) and all(c.shape == w.shape for c, w in zip(cast, (wg, wu, wd)))
        span = d // tiles
        piece_shapes = [(span, tf), (span, tf), (tf, span)]
        in_specs += [any_spec] * 3
        out_specs += [any_spec] * 3
        out_shape += [jax.ShapeDtypeStruct(c.shape, BF16) for c in cast]
        scratch += [pltpu.VMEM((2,) + s, F32) for s in piece_shapes]
        scratch += [pltpu.VMEM((2,) + s, BF16) for s in piece_shapes]
        scratch += [pltpu.SemaphoreType.DMA((3, 2)), pltpu.SemaphoreType.DMA((3, 2))]
        operands += list(cast)
    body = functools.partial(_ffn_body, n_chunks=n_chunks, final_norm=final_norm, side_cast=side_cast)
    out = pl.pallas_call(
        body,
        grid=(tiles,),
        in_specs=in_specs,
        out_specs=out_specs,
        out_shape=out_shape,
        scratch_shapes=scratch,
        compiler_params=_params("arbitrary"),
        name="ffn",
    )(*operands)
    return out if side_cast else out[0]


def _in_proj_body(x_hbm, g_ref, w_ref, b_ref, o_ref, x_buf, a_ref, sem):
    i = pl.program_id(0)
    tm = x_buf.shape[0]

    def x_in(t):
        return pltpu.make_async_copy(x_hbm.at[pl.ds(pl.multiple_of(t * tm, tm), tm), :], x_buf, sem.at[0])

    @pl.when(pl.program_id(1) == 0)
    def _():
        @pl.when(i == 0)
        def _():
            x_in(0).start()

        x_in(i).wait()

        def norm_rows(r, carry):
            rows = pl.ds(pl.multiple_of(r * FFN_ROW_CHUNK, FFN_ROW_CHUNK), FFN_ROW_CHUNK)
            a_ref[rows, :] = _rms_rows(x_buf[rows, :], g_ref[...]).astype(BF16)
            return carry

        lax.fori_loop(0, tm // FFN_ROW_CHUNK, norm_rows, 0)

        @pl.when(i + 1 < pl.num_programs(0))
        def _():
            x_in(i + 1).start()

    acc = jnp.dot(a_ref[...], w_ref[...], preferred_element_type=F32) + b_ref[...]
    for c in range(o_ref.shape[0]):
        o_ref[c] = acc[:, c * LANES:(c + 1) * LANES].astype(o_ref.dtype)


def _in_proj(x, g, w, b):
    m, k = x.shape
    n = w.shape[1]
    tm = _tile(m, PROJ_ROWS)
    tn = _tile(n, PROJ_COLS)
    assert tm % FFN_ROW_CHUNK == 0 and tn % LANES == 0
    return pl.pallas_call(
        _in_proj_body,
        grid=(m // tm, n // tn),
        in_specs=[
            pl.BlockSpec(memory_space=pl.ANY),
            pl.BlockSpec((1, k), lambda i, j: (0, 0)),
            pl.BlockSpec((k, tn), lambda i, j: (0, j)),
            pl.BlockSpec((1, tn), lambda i, j: (0, j)),
        ],
        out_specs=pl.BlockSpec((tn // LANES, tm, LANES), lambda i, j: (j, i, 0)),
        out_shape=jax.ShapeDtypeStruct((n // LANES, m, LANES), BF16),
        scratch_shapes=[pltpu.VMEM((tm, k), F32), pltpu.VMEM((tm, k), BF16), pltpu.SemaphoreType.DMA((1,))],
        compiler_params=_params("arbitrary", "arbitrary"),
        name="in_proj",
    )(x, g.reshape(1, k), w, b.reshape(1, n))


def _matmul_body(a_ref, w_ref, o_ref):
    o_ref[...] = jnp.dot(a_ref[...], w_ref[...], preferred_element_type=F32).astype(o_ref.dtype)


def _matmul(a, w, out_dtype):
    m, k = a.shape
    n = w.shape[1]
    tm = _tile(m, PROJ_ROWS)
    tn = _tile(n, KV_COLS)
    return pl.pallas_call(
        _matmul_body,
        grid=(m // tm, n // tn),
        in_specs=[pl.BlockSpec((tm, k), lambda i, j: (i, 0)), pl.BlockSpec((k, tn), lambda i, j: (0, j))],
        out_specs=pl.BlockSpec((tm, tn), lambda i, j: (i, j)),
        out_shape=jax.ShapeDtypeStruct((m, n), out_dtype),
        compiler_params=_params("parallel", "parallel"),
        name="matmul",
    )(a, w)


def _matmul_residual_body(a_ref, w_ref, r_ref, o_ref):
    o_ref[...] = r_ref[...] + jnp.dot(a_ref[...], w_ref[...], preferred_element_type=F32)


def _matmul_residual(a, w, res):
    m, k = a.shape
    n = w.shape[1]
    tm = _tile(m, PROJ_ROWS)
    tn = _tile(n, PROJ_COLS)
    return pl.pallas_call(
        _matmul_residual_body,
        grid=(m // tm, n // tn),
        in_specs=[
            pl.BlockSpec((tm, k), lambda i, j: (i, 0)),
            pl.BlockSpec((k, tn), lambda i, j: (0, j)),
            pl.BlockSpec((tm, tn), lambda i, j: (i, j)),
        ],
        out_specs=pl.BlockSpec((tm, tn), lambda i, j: (i, j)),
        out_shape=jax.ShapeDtypeStruct((m, n), F32),
        compiler_params=_params("parallel", "parallel"),
        name="out_proj",
    )(a, w, res)


def _fold_rows(x, op, rows=8):
    while x.shape[0] > rows:
        half = x.shape[0] // 2
        x = op(x[:half], x[half:])
    return x


def _moba_body(q_ref, k_ref, v_ref, town_ref, tprev_ref, bfar_ref, o_ref, km_ref, sel_ref, s_ref, l_ref, acc_ref,
               *, n_blocks, topk, heads):
    n = pl.program_id(2)
    blk = MOBA_BLOCK
    pair = 2 * blk
    scale = SCORE_SCALE
    group = range(heads)

    @pl.when(n == 0)
    def _():
        for g in group:
            for j in range(n_blocks):
                kj = k_ref[g, 0, j * blk:(j + 1) * blk, :].astype(F32)
                km_ref[g, j:j + 1, :] = jnp.mean(kj, axis=0, keepdims=True)

    q = [q_ref[g, 0] for g in group]

    for g in group:
        gate = lax.dot_general(km_ref[g].astype(BF16), q[g], _NT, preferred_element_type=F32)
        rows = lax.broadcasted_iota(jnp.int32, gate.shape, 0)
        gate = jnp.where(rows < n, gate, -jnp.inf)
        sel = jnp.zeros(gate.shape, jnp.bool_)
        for _ in range(topk):
            best = jnp.max(gate, axis=0, keepdims=True)
            first = jnp.min(jnp.where(gate == best, rows, n_blocks), axis=0, keepdims=True)
            pick = (rows == first) & (best > -jnp.inf)
            sel = sel | pick
            gate = jnp.where(pick, -jnp.inf, gate)
        sel_ref[g] = jnp.where(sel, 0.0, MASKED)

    def scores(g, rows):
        return lax.dot_general(k_ref[g, 0, rows, :], q[g], _NT, preferred_element_type=F32) * scale

    def pv(g, rows, p):
        return lax.dot_general(v_ref[g, 0, rows, :], p.astype(BF16), _TN, preferred_element_type=F32)

    jp = jnp.maximum(n - 1, 0)
    own_rows = pl.ds(pl.multiple_of(n * blk, blk), blk)
    prev_rows = pl.ds(pl.multiple_of(jp * blk, blk), blk)
    m8 = []
    for g in group:
        s_own = scores(g, own_rows) + town_ref[g]
        s_prev = scores(g, prev_rows) + tprev_ref[g] + sel_ref[g, pl.ds(jp, 1), :]
        s_ref[g, 0:blk, :] = s_own
        s_ref[g, blk:pair, :] = s_prev
        m8.append(jnp.maximum(_fold_rows(s_own, jnp.maximum), _fold_rows(s_prev, jnp.maximum)))

    n_far = n - 1
    n_pairs = lax.shift_right_logical(n, 1)

    def far_scores(c, m8):
        key_rows = pl.ds(pl.multiple_of(c * pair, pair), pair)
        base = pl.multiple_of(pair + c * pair, pair)
        j1 = 2 * c + 1
        out = []
        for g in group:
            bfar = bfar_ref[g]
            s0 = scores(g, pl.ds(pl.multiple_of(c * pair, blk), blk)) + (sel_ref[g, pl.ds(2 * c, 1), :] + bfar)
            s1 = scores(g, pl.ds(pl.multiple_of(c * pair + blk, blk), blk)) + jnp.where(
                j1 < n_far, sel_ref[g, pl.ds(j1, 1), :] + bfar, MASKED)
            s_ref[g, pl.ds(base, blk), :] = s0
            s_ref[g, pl.ds(base + blk, blk), :] = s1
            out.append(jnp.maximum(m8[g], jnp.maximum(_fold_rows(s0, jnp.maximum), _fold_rows(s1, jnp.maximum))))
        return tuple(out)

    m8 = lax.fori_loop(0, n_pairs, far_scores, tuple(m8))
    m = [jnp.max(m8[g], axis=0, keepdims=True) for g in group]

    for g in group:
        p_own = jnp.exp2(s_ref[g, 0:blk, :] - m[g])
        p_prev = jnp.exp2(s_ref[g, blk:pair, :] - m[g])
        l_ref[g] = _fold_rows(p_own, jnp.add) + _fold_rows(p_prev, jnp.add)
        acc_ref[g] = pv(g, own_rows, p_own) + pv(g, prev_rows, p_prev)

    def far_pv(c, carry):
        key_rows = pl.ds(pl.multiple_of(c * pair, pair), pair)
        base = pl.multiple_of(pair + c * pair, pair)
        for g in group:
            part = 0.0
            for half in range(2):
                rows = pl.ds(pl.multiple_of(c * pair + half * blk, blk), blk)
                p = jnp.exp2(s_ref[g, pl.ds(base + half * blk, blk), :] - m[g])
                l_ref[g] += _fold_rows(p, jnp.add)
                part = part + pv(g, rows, p)
            acc_ref[g] += part
        return carry

    lax.fori_loop(0, n_pairs, far_pv, 0)
    for g in group:
        l = jnp.sum(l_ref[g], axis=0, keepdims=True)
        o_ref[g, 0] = (acc_ref[g] / l).T.astype(o_ref.dtype)


def _moba(proj, town, tprev, bfar):
    _, b, s, _ = proj.shape
    blk = MOBA_BLOCK
    n_blocks = s // blk
    heads = MOBA_HEADS_PER_STEP
    body = functools.partial(_moba_body, n_blocks=n_blocks, topk=min(MOBA_TOPK, n_blocks), heads=heads)
    return pl.pallas_call(
        body,
        grid=(b, N_MOBA_HEADS // heads, n_blocks),
        in_specs=[
            pl.BlockSpec((heads, 1, blk, HEAD_DIM), lambda bi, h, n: (SLAB_QM // heads + h, bi, n, 0)),
            pl.BlockSpec((heads, 1, s, HEAD_DIM), lambda bi, h, n: (SLAB_KM // heads + h, bi, 0, 0)),
            pl.BlockSpec((heads, 1, s, HEAD_DIM), lambda bi, h, n: (SLAB_VM // heads + h, bi, 0, 0)),
            pl.BlockSpec((heads, blk, blk), lambda bi, h, n: (h, 0, 0)),
            pl.BlockSpec((heads, blk, blk), lambda bi, h, n: (h, 0, 0)),
            pl.BlockSpec((heads, 1, blk), lambda bi, h, n: (h, 0, 0)),
        ],
        out_specs=pl.BlockSpec((heads, 1, blk, HEAD_DIM), lambda bi, h, n: (h, bi, n, 0)),
        out_shape=jax.ShapeDtypeStruct((N_MOBA_HEADS, b, s, HEAD_DIM), F32),
        scratch_shapes=[
            pltpu.VMEM((heads, n_blocks, HEAD_DIM), F32),
            pltpu.VMEM((heads, n_blocks, blk), F32),
            pltpu.VMEM((heads, 2 * blk * (1 + (n_blocks - 1) // 2), blk), F32),
            pltpu.VMEM((heads, 8, blk), F32),
            pltpu.VMEM((heads, HEAD_DIM, blk), F32),
        ],
        compiler_params=_params("parallel", "parallel", "arbitrary"),
        name="moba",
    )(proj, proj, proj, town, tprev, bfar)


def _swa_body(q_ref, k_ref, v_ref, bias_ref, sink_ref, o_ref, *, blocks_per_step):
    blk = SWA_BLOCK
    scale = SCORE_SCALE
    sink = sink_ref[0]
    blocks = range(blocks_per_step)

    def band(ref, r):
        n = pl.program_id(2) * blocks_per_step + r
        prev_rows = pl.ds(pl.multiple_of(jnp.maximum(n - 1, 0) * blk, blk), blk)
        own_rows = pl.ds(pl.multiple_of(n * blk, blk), blk)
        return jnp.concatenate([ref[0, 0, prev_rows, :], ref[0, 0, own_rows, :]], axis=0)

    raw = []
    for r in blocks:
        q = jnp.concatenate([q_ref[g, 0, r * blk:(r + 1) * blk, :] for g in range(SWA_GROUP)], axis=0)
        raw.append(lax.dot_general(band(k_ref, r), q, _NT, preferred_element_type=F32))
    soft = []
    for r in blocks:
        n = pl.program_id(2) * blocks_per_step + r
        s = raw[r] * scale + bias_ref[0]
        key_row = lax.broadcasted_iota(jnp.int32, s.shape, 0)
        s = jnp.where((key_row < blk) & (n == 0), MASKED, s)
        m = jnp.maximum(jnp.max(s, axis=0, keepdims=True), sink)
        p = jnp.exp2(s - m)
        soft.append((p.astype(BF16), jnp.sum(p, axis=0, keepdims=True) + jnp.exp2(sink - m)))
    for r in blocks:
        p, l = soft[r]
        o = lax.dot_general(band(v_ref, r), p, _TN, preferred_element_type=F32) / l
        for g in range(SWA_GROUP):
            o_ref[g, 0, r * blk:(r + 1) * blk, :] = o[:, g * blk:(g + 1) * blk].T.astype(o_ref.dtype)


def _swa(proj, bias, sink):
    _, b, s, _ = proj.shape
    blk = SWA_BLOCK
    gq = SWA_GROUP * blk
    per_step = SWA_BLOCKS_PER_STEP if (s // blk) % SWA_BLOCKS_PER_STEP == 0 else 1
    rows = per_step * blk
    return pl.pallas_call(
        functools.partial(_swa_body, blocks_per_step=per_step),
        grid=(b, N_SWA_KV_HEADS, s // rows),
        in_specs=[
            pl.BlockSpec((SWA_GROUP, 1, rows, HEAD_DIM), lambda bi, h, n: (SLAB_QS // SWA_GROUP + h, bi, n, 0)),
            pl.BlockSpec((1, 1, s, HEAD_DIM), lambda bi, h, n: (SLAB_KS + h, bi, 0, 0)),
            pl.BlockSpec((1, 1, s, HEAD_DIM), lambda bi, h, n: (SLAB_VS + h, bi, 0, 0)),
            pl.BlockSpec((1, 2 * blk, gq), lambda bi, h, n: (h, 0, 0)),
            pl.BlockSpec((1, 1, gq), lambda bi, h, n: (h, 0, 0)),
        ],
        out_specs=pl.BlockSpec((SWA_GROUP, 1, rows, HEAD_DIM), lambda bi, h, n: (h, bi, n, 0)),
        out_shape=jax.ShapeDtypeStruct((N_SWA_HEADS, b, s, HEAD_DIM), F32),
        compiler_params=_params("parallel", "parallel", "arbitrary"),
        name="swa",
    )(proj, proj, proj, bias, sink)


def _head_norm_body(om_ref, os_ref, gm_ref, gs_ref, o_ref):
    col = 0
    for x_ref, g_ref in ((om_ref, gm_ref), (os_ref, gs_ref)):
        heads = x_ref.shape[0]
        ss = sum(jnp.sum(jnp.square(x_ref[c]), axis=-1, keepdims=True) for c in range(heads))
        r = lax.rsqrt(ss / (heads * HEAD_DIM) + EPS)
        for c in range(heads):
            o_ref[:, col:col + HEAD_DIM] = (x_ref[c] * r * g_ref[:, c * HEAD_DIM:(c + 1) * HEAD_DIM]).astype(o_ref.dtype)
            col += HEAD_DIM


def _head_norm(om, osw, gm, gs):
    hm, m, _ = om.shape
    hs = osw.shape[0]
    tm = _tile(m, HEAD_NORM_ROWS)
    width = (hm + hs) * HEAD_DIM
    return pl.pallas_call(
        _head_norm_body,
        grid=(m // tm,),
        in_specs=[
            pl.BlockSpec((hm, tm, HEAD_DIM), lambda i: (0, i, 0)),
            pl.BlockSpec((hs, tm, HEAD_DIM), lambda i: (0, i, 0)),
            pl.BlockSpec((1, hm * HEAD_DIM), lambda i: (0, 0)),
            pl.BlockSpec((1, hs * HEAD_DIM), lambda i: (0, 0)),
        ],
        out_specs=pl.BlockSpec((tm, width), lambda i: (i, 0)),
        out_shape=jax.ShapeDtypeStruct((m, width), BF16),
        compiler_params=_params("parallel"),
        name="head_norm",
    )(om, osw, gm.reshape(1, -1), gs.reshape(1, -1))


def _cross_body(x_ref, g_ref, wq_ref, kv_ref, wo_ref, o_ref, h_ref):
    scale = SCORE_SCALE
    tm = x_ref.shape[0]

    def norm_rows(r, carry):
        rows = pl.ds(pl.multiple_of(r * FFN_ROW_CHUNK, FFN_ROW_CHUNK), FFN_ROW_CHUNK)
        h_ref[rows, :] = _rms_rows(x_ref[rows, :], g_ref[...]).astype(BF16)
        return carry

    lax.fori_loop(0, tm // FFN_ROW_CHUNK, norm_rows, 0)
    q = jnp.dot(h_ref[...], wq_ref[...], preferred_element_type=F32).astype(BF16)
    outs = []
    for hd in range(N_CROSS_HEADS):
        cols = slice(hd * HEAD_DIM, (hd + 1) * HEAD_DIM)
        k = kv_ref[0, :, cols]
        v = kv_ref[0, :, CROSS_W + hd * HEAD_DIM:CROSS_W + (hd + 1) * HEAD_DIM]
        s = lax.dot_general(q[:, cols], k, _NT, preferred_element_type=F32) * scale
        p = jnp.exp2(s - jnp.max(s, axis=-1, keepdims=True))
        l = jnp.sum(p, axis=-1, keepdims=True)
        outs.append((jnp.dot(p.astype(BF16), v, preferred_element_type=F32) / l).astype(BF16))
    o = jnp.concatenate(outs, axis=-1)
    o_ref[...] = x_ref[...] + jnp.dot(o, wo_ref[...], preferred_element_type=F32)


def _cross(x, g, wq, kv, wo, seq):
    m, d = x.shape
    n_mem = kv.shape[1]
    tm = _tile(seq, CROSS_ROWS)
    assert tm % FFN_ROW_CHUNK == 0
    tiles_per_seq = seq // tm
    resident = pl.Buffered(1)
    return pl.pallas_call(
        _cross_body,
        grid=(m // tm,),
        in_specs=[
            pl.BlockSpec((tm, d), lambda i: (i, 0)),
            pl.BlockSpec((1, d), lambda i: (0, 0)),
            pl.BlockSpec((d, CROSS_W), lambda i: (0, 0), pipeline_mode=resident),
            pl.BlockSpec((1, n_mem, 2 * CROSS_W), lambda i: (i // tiles_per_seq, 0, 0)),
            pl.BlockSpec((CROSS_W, d), lambda i: (0, 0), pipeline_mode=resident),
        ],
        out_specs=pl.BlockSpec((tm, d), lambda i: (i, 0)),
        out_shape=jax.ShapeDtypeStruct((m, d), F32),
        scratch_shapes=[pltpu.VMEM((tm, d), BF16)],
        compiler_params=_params("parallel"),
        name="cross",
    )(x, g.reshape(1, d), wq, kv, wo)


def _t5_bucket(dist):
    n = np.maximum(dist, 0)
    max_exact = NUM_BUCKETS // 2
    nf = np.maximum(n, max_exact).astype(np.float32)
    large = max_exact + (np.log(nf / np.float32(max_exact)) / np.float32(math.log(MAX_DISTANCE / max_exact))
                         * np.float32(NUM_BUCKETS - max_exact)).astype(np.int32)
    large = np.minimum(large, NUM_BUCKETS - 1)
    return np.where(n < max_exact, n, large).astype(np.int32)


def _bias_by_distance(bias_hb, dists):
    onehot = _t5_bucket(dists)[None, :] == np.arange(NUM_BUCKETS)[:, None]
    return jnp.sum(jnp.where(onehot[None], bias_hb[:, :, None], 0.0), axis=1)


def _toeplitz(g, rows, cols):
    heads, n = g.shape
    assert n == rows + cols - 1
    width = rows + cols
    tiled = jnp.broadcast_to(jnp.pad(g, ((0, 0), (0, 1)))[:, None, :], (heads, rows, width))
    skew = tiled.reshape(heads, rows * width)[:, :rows * (width - 1)].reshape(heads, rows, width - 1)
    return skew[:, :, rows - 1:rows - 1 + cols]


def _moba_bias_tables(bias_hb):
    blk = MOBA_BLOCK
    d_own = np.arange(2 * blk - 1) - (blk - 1)
    g_own = jnp.where(d_own >= 0, _bias_by_distance(bias_hb, np.maximum(d_own, 0)), MASKED)
    g_prev = _bias_by_distance(bias_hb, d_own + blk)
    far = int(_t5_bucket(np.array(blk + 1)))
    bfar = jnp.broadcast_to(bias_hb[:, far][:, None, None], (bias_hb.shape[0], 1, blk))
    return _toeplitz(g_own, blk, blk), _toeplitz(g_prev, blk, blk), bfar


def _swa_bias_table(bias_hb):
    blk = SWA_BLOCK
    dist = np.arange(3 * blk - 1) - (blk - 1)
    ok = (dist >= 0) & (dist < SWA_WINDOW)
    g = jnp.where(ok, _bias_by_distance(bias_hb, np.maximum(dist, 0)), MASKED)
    t = _toeplitz(g, 2 * blk, blk)
    t = t.reshape(N_SWA_KV_HEADS, SWA_GROUP, 2 * blk, blk).transpose(0, 2, 1, 3)
    return t.reshape(N_SWA_KV_HEADS, 2 * blk, SWA_GROUP * blk)


def kernel(x, mem, rel_bias, g_final, g_ffn1, w1_gate, w1_up, w1_down, g_mix, w_in, b_in, sinks, g_out_moba, g_out_swa, w_out, g_cross, g_mem, w_cq, w_ck, w_cv, w_co, g_ffn2, w2_gate, w2_up, w2_down):
    b, s, d = x.shape
    n_mem = mem.shape[1]
    depth = w_in.shape[0]
    assert s % MOBA_BLOCK == 0 and s % SWA_BLOCK == 0 and d % LANES == 0

    bias_moba = rel_bias[:, :N_MOBA_HEADS].T.astype(F32)
    bias_swa = rel_bias[:, N_MOBA_HEADS:].T.astype(F32)
    town, tprev, bfar = _moba_bias_tables(bias_moba * LOG2E)
    tswa = _swa_bias_table(bias_swa * LOG2E)

    xt = x.reshape(b * s, d)
    memt = mem.reshape(b * n_mem, d)
    for l in range(depth):
        w1 = [w.astype(BF16) for w in (w1_gate[l], w1_up[l], w1_down[l])]
        w2 = (w2_gate[l], w2_up[l], w2_down[l])
        if _can_side_cast(b * s, d, w1[0].shape[1]) and all(p.shape == q.shape for p, q in zip(w1, w2)):
            xt, *w2 = _ffn(xt, g_ffn1[l], *w1, cast=w2)
        else:
            xt = _ffn(xt, g_ffn1[l], *w1)
            w2 = [w.astype(BF16) for w in w2]

        proj = _in_proj(xt, g_mix[l], w_in[l].astype(BF16), b_in[l]).reshape(N_SLABS, b, s, HEAD_DIM)
        o_m = _moba(proj, town, tprev, bfar).reshape(N_MOBA_HEADS, b * s, HEAD_DIM)
        sink = jnp.broadcast_to((sinks[l].astype(F32) * LOG2E).reshape(N_SWA_KV_HEADS, SWA_GROUP, 1),
                                (N_SWA_KV_HEADS, SWA_GROUP, SWA_BLOCK)).reshape(N_SWA_KV_HEADS, 1, SWA_GROUP * SWA_BLOCK)
        o_s = _swa(proj, tswa, sink).reshape(N_SWA_HEADS, b * s, HEAD_DIM)
        o = _head_norm(o_m, o_s, g_out_moba[l], g_out_swa[l])
        xt = _matmul_residual(o, w_out[l].astype(BF16), xt)

        mem_n = _rmsnorm(memt, g_mem[l], BF16)
        w_kv = jnp.concatenate([w_ck[l], w_cv[l]], axis=1).astype(BF16)
        kv = _matmul(mem_n, w_kv, BF16).reshape(b, n_mem, 2 * CROSS_W)
        xt = _cross(xt, g_cross[l], w_cq[l].astype(BF16), kv, w_co[l].astype(BF16), s)

        xt = _ffn(xt, g_ffn2[l], *w2, g_out=g_final if l == depth - 1 else None)
    return xt.reshape(b, s, d)
```

```python
import functools
import math

import numpy as np
import jax
import jax.numpy as jnp
from jax import lax
from jax.experimental import pallas as pl
from jax.experimental.pallas import tpu as pltpu

HEAD_DIM = 128
N_MOBA_HEADS = 16
N_SWA_HEADS = 16
N_SWA_KV_HEADS = 4
SWA_GROUP = N_SWA_HEADS // N_SWA_KV_HEADS
MOBA_BLOCK = 256
MOBA_TOPK = 3
SWA_WINDOW = 128
SWA_BLOCK = 128
N_CROSS_HEADS = 4
NUM_BUCKETS = 32
MAX_DISTANCE = 128
EPS = 1e-6

CROSS_W = N_CROSS_HEADS * HEAD_DIM

SLAB_QM = 0
SLAB_KM = SLAB_QM + N_MOBA_HEADS
SLAB_VM = SLAB_KM + N_MOBA_HEADS
SLAB_QS = SLAB_VM + N_MOBA_HEADS
SLAB_KS = SLAB_QS + N_SWA_HEADS
SLAB_VS = SLAB_KS + N_SWA_KV_HEADS
N_SLABS = SLAB_VS + N_SWA_KV_HEADS

MASKED = -1e30
LOG2E = math.log2(math.e)
SCORE_SCALE = HEAD_DIM ** -0.5 * LOG2E
LANES = 128
MXU_DIM = 256
V7X_VMEM_LIMIT_BYTES = 56 * 1024 * 1024

FFN_ROWS = 1024
FFN_CHUNK = MXU_DIM
PROJ_ROWS = 1024
PROJ_COLS = 1024
KV_COLS = 512
NORM_ROWS = 256
HEAD_NORM_ROWS = 512
CROSS_ROWS = 512
MOBA_HEADS_PER_STEP = 4
SWA_BLOCKS_PER_STEP = 8

F32 = jnp.float32
BF16 = jnp.bfloat16

_NT = (((1,), (1,)), ((), ()))
_TN = (((0,), (0,)), ((), ()))


def _tile(n, want):
    if n <= want:
        return n
    t = want
    while n % t:
        t -= 8
    return t


def _params(*semantics):
    return pltpu.CompilerParams(dimension_semantics=semantics, vmem_limit_bytes=V7X_VMEM_LIMIT_BYTES)


def _rms_rows(x, g):
    ms = jnp.mean(x * x, axis=-1, keepdims=True)
    return x * lax.rsqrt(ms + EPS) * g


def _rmsnorm_body(x_ref, g_ref, o_ref):
    o_ref[...] = _rms_rows(x_ref[...], g_ref[...]).astype(o_ref.dtype)


def _rmsnorm(x, g, out_dtype):
    m, d = x.shape
    tm = _tile(m, NORM_ROWS)
    return pl.pallas_call(
        _rmsnorm_body,
        grid=(m // tm,),
        in_specs=[pl.BlockSpec((tm, d), lambda i: (i, 0)), pl.BlockSpec((1, d), lambda i: (0, 0))],
        out_specs=pl.BlockSpec((tm, d), lambda i: (i, 0)),
        out_shape=jax.ShapeDtypeStruct((m, d), out_dtype),
        compiler_params=_params("parallel"),
        name="rmsnorm",
    )(x, g.reshape(1, d))


FFN_ROW_CHUNK = 64


def _ffn_body(*refs, n_chunks, final_norm, side_cast):
    x_hbm, g_ref, gout_ref, wg_hbm, wu_hbm, wd_hbm = refs[:6]
    refs = refs[6:]
    if side_cast:
        src_hbm, refs = refs[:3], refs[3:]
    o_hbm, refs = refs[0], refs[1:]
    if side_cast:
        dst_hbm, refs = refs[:3], refs[3:]
    tile_buf, h_ref, wg_buf, wu_buf, wd_buf, sem, x_sem, o_sem = refs[:8]
    if side_cast:
        cin, cout, cin_sem, cout_sem = refs[8:11], refs[11:14], refs[14], refs[15]
    i = pl.program_id(0)
    n_tiles = pl.num_programs(0)
    tm = tile_buf.shape[1]
    tf = wd_buf.shape[1]
    last = n_chunks - 1
    backwards = lax.rem(i, 2) == 1
    cur = lax.rem(i, 2)
    o_ref = tile_buf.at[cur]

    def tile_rows(t):
        return pl.ds(pl.multiple_of(t * tm, tm), tm)

    def x_in(t, slot):
        return pltpu.make_async_copy(x_hbm.at[tile_rows(t), :], tile_buf.at[slot], x_sem.at[slot])

    def o_out(t, slot):
        return pltpu.make_async_copy(tile_buf.at[slot], o_hbm.at[tile_rows(t), :], o_sem.at[slot])

    @pl.when(i == 0)
    def _():
        x_in(0, 0).start()

    def fetch(pos, slot):
        cols = pl.ds(pl.multiple_of(jnp.where(backwards, last - pos, pos) * tf, tf), tf)
        return (pltpu.make_async_copy(wg_hbm.at[:, cols], wg_buf.at[slot], sem.at[0, slot]),
                pltpu.make_async_copy(wu_hbm.at[:, cols], wu_buf.at[slot], sem.at[1, slot]),
                pltpu.make_async_copy(wd_hbm.at[cols, :], wd_buf.at[slot], sem.at[2, slot]))

    def start(pos, slot):
        for copy in fetch(pos, slot):
            copy.start()

    def wait(pos, slot):
        for copy in fetch(pos, slot):
            copy.wait()

    def pieces(hbm, pos):
        span = cin[0].shape[1]
        own = pl.ds(pl.multiple_of(i * span, span), span)
        chunk = pl.ds(pl.multiple_of(pos * tf, tf), tf)
        return hbm[0].at[own, chunk], hbm[1].at[own, chunk], hbm[2].at[chunk, own]

    def cast_in(pos, slot):
        return [pltpu.make_async_copy(src, cin[k].at[slot], cin_sem.at[k, slot])
                for k, src in enumerate(pieces(src_hbm, pos))]

    def cast_out(pos, slot):
        return [pltpu.make_async_copy(cout[k].at[slot], dst, cout_sem.at[k, slot])
                for k, dst in enumerate(pieces(dst_hbm, pos))]

    def side(copies, method, when=None):
        if not side_cast:
            return

        def go():
            for copy in copies():
                getattr(copy, method)()

        if when is None:
            go()
        else:
            pl.when(when)(go)

    @pl.when(i == 0)
    def _():
        start(0, 0)

    side(lambda: cast_in(0, 0), "start")
    x_in(i, cur).wait()

    def norm_rows(r, carry):
        rows = pl.ds(pl.multiple_of(r * FFN_ROW_CHUNK, FFN_ROW_CHUNK), FFN_ROW_CHUNK)
        h_ref[rows, :] = _rms_rows(o_ref[rows, :], g_ref[...]).astype(BF16)
        return carry

    lax.fori_loop(0, tm // FFN_ROW_CHUNK, norm_rows, 0)

    @pl.when(i == 0)
    def _():
        wait(0, 0)

    def compute(slot):
        if side_cast:
            for k in range(3):
                cout[k][slot] = cin[k][slot].astype(BF16)
        h = h_ref[...]
        gate = jnp.dot(h, wg_buf[slot], preferred_element_type=F32)
        up = jnp.dot(h, wu_buf[slot], preferred_element_type=F32)
        act = (0.5 * gate * jax.nn.sigmoid(gate) * up).astype(BF16)
        o_ref[...] += jnp.dot(act, wd_buf[slot], preferred_element_type=F32)

    def chunk_pair(pp, carry):
        pos = 2 * pp

        @pl.when((pp == 1) & (i >= 1))
        def _():
            o_out(i - 1, 1 - cur).wait()

        @pl.when((pp == 1) & (i + 1 < n_tiles))
        def _():
            x_in(i + 1, 1 - cur).start()

        side(lambda: cast_in(pos, 0), "wait")
        side(lambda: cast_out(pos - 1, 1), "start", pos >= 1)
        side(lambda: cast_out(pos - 2, 0), "wait", pos >= 2)

        @pl.when(pos < last)
        def _():
            start(pos + 1, 1)
            side(lambda: cast_in(pos + 1, 1), "start")

        compute(0)

        @pl.when(pos < last)
        def _():
            wait(pos + 1, 1)
            side(lambda: cast_in(pos + 1, 1), "wait")
            side(lambda: cast_out(pos, 0), "start")
            side(lambda: cast_out(pos - 1, 1), "wait", pos >= 1)
            start(pos + 2, 0)
            side(lambda: cast_in(pos + 2, 0), "start")
            compute(1)
            wait(pos + 2, 0)

        return carry

    lax.fori_loop(0, (n_chunks + 1) // 2, chunk_pair, 0)
    side(lambda: cast_out(last, 0), "start")
    side(lambda: cast_out(last - 1, 1), "wait")
    side(lambda: cast_out(last, 0), "wait")

    if final_norm:
        def out_rows(r, carry):
            rows = pl.ds(pl.multiple_of(r * FFN_ROW_CHUNK, FFN_ROW_CHUNK), FFN_ROW_CHUNK)
            o_ref[rows, :] = _rms_rows(o_ref[rows, :], gout_ref[...])
            return carry

        lax.fori_loop(0, tm // FFN_ROW_CHUNK, out_rows, 0)

    o_out(i, cur).start()

    @pl.when(i == n_tiles - 1)
    def _():
        o_out(i, cur).wait()


def _ffn_tiles(m, d, f):
    tm = _tile(m, FFN_ROWS)
    tf = FFN_CHUNK if f % FFN_CHUNK == 0 else LANES
    return tm, tf


def _can_side_cast(m, d, f):
    tm, tf = _ffn_tiles(m, d, f)
    tiles = m // tm
    return d % tiles == 0 and (d // tiles) % LANES == 0


def _ffn(x, g, wg, wu, wd, g_out=None, cast=None):
    m, d = x.shape
    f = wg.shape[1]
    tm, tf = _ffn_tiles(m, d, f)
    n_chunks = f // tf
    tiles = m // tm
    assert f % tf == 0 and n_chunks % 2 == 1 and n_chunks >= 3 and tm % FFN_ROW_CHUNK == 0
    final_norm = g_out is not None
    side_cast = cast is not None
    any_spec = pl.BlockSpec(memory_space=pl.ANY)
    in_specs = [
        any_spec,
        pl.BlockSpec((1, d), lambda i: (0, 0)),
        pl.BlockSpec((1, d), lambda i: (0, 0)),
        any_spec, any_spec, any_spec,
    ]
    out_specs = [any_spec]
    out_shape = [jax.ShapeDtypeStruct((m, d), F32)]
    scratch = [
        pltpu.VMEM((2, tm, d), F32),
        pltpu.VMEM((tm, d), BF16),
        pltpu.VMEM((2, d, tf), BF16),
        pltpu.VMEM((2, d, tf), BF16),
        pltpu.VMEM((2, tf, d), BF16),
        pltpu.SemaphoreType.DMA((3, 2)),
        pltpu.SemaphoreType.DMA((2,)),
        pltpu.SemaphoreType.DMA((2,)),
    ]
    operands = [x, g.reshape(1, d), (g_out if final_norm else g).reshape(1, d), wg, wu, wd]
    if side_cast:
        assert _can_side_cast(m, d, f) and all(c.shape == w.shape for c, w in zip(cast, (wg, wu, wd)))
        span = d // tiles
        piece_shapes = [(span, tf), (span, tf), (tf, span)]
        in_specs += [any_spec] * 3
        out_specs += [any_spec] * 3
        out_shape += [jax.ShapeDtypeStruct(c.shape, BF16) for c in cast]
        scratch += [pltpu.VMEM((2,) + s, F32) for s in piece_shapes]
        scratch += [pltpu.VMEM((2,) + s, BF16) for s in piece_shapes]
        scratch += [pltpu.SemaphoreType.DMA((3, 2)), pltpu.SemaphoreType.DMA((3, 2))]
        operands += list(cast)
    body = functools.partial(_ffn_body, n_chunks=n_chunks, final_norm=final_norm, side_cast=side_cast)
    out = pl.pallas_call(
        body,
        grid=(tiles,),
        in_specs=in_specs,
        out_specs=out_specs,
        out_shape=out_shape,
        scratch_shapes=scratch,
        compiler_params=_params("arbitrary"),
        name="ffn",
    )(*operands)
    return out if side_cast else out[0]


def _in_proj_body(x_hbm, g_ref, w_ref, b_ref, o_ref, x_buf, a_ref, sem):
    i = pl.program_id(0)
    tm = x_buf.shape[0]

    def x_in(t):
        return pltpu.make_async_copy(x_hbm.at[pl.ds(pl.multiple_of(t * tm, tm), tm), :], x_buf, sem.at[0])

    @pl.when(pl.program_id(1) == 0)
    def _():
        @pl.when(i == 0)
        def _():
            x_in(0).start()

        x_in(i).wait()

        def norm_rows(r, carry):
            rows = pl.ds(pl.multiple_of(r * FFN_ROW_CHUNK, FFN_ROW_CHUNK), FFN_ROW_CHUNK)
            a_ref[rows, :] = _rms_rows(x_buf[rows, :], g_ref[...]).astype(BF16)
            return carry

        lax.fori_loop(0, tm // FFN_ROW_CHUNK, norm_rows, 0)

        @pl.when(i + 1 < pl.num_programs(0))
        def _():
            x_in(i + 1).start()

    acc = jnp.dot(a_ref[...], w_ref[...], preferred_element_type=F32) + b_ref[...]
    for c in range(o_ref.shape[0]):
        o_ref[c] = acc[:, c * LANES:(c + 1) * LANES].astype(o_ref.dtype)


def _in_proj(x, g, w, b):
    m, k = x.shape
    n = w.shape[1]
    tm = _tile(m, PROJ_ROWS)
    tn = _tile(n, PROJ_COLS)
    assert tm % FFN_ROW_CHUNK == 0 and tn % LANES == 0
    return pl.pallas_call(
        _in_proj_body,
        grid=(m // tm, n // tn),
        in_specs=[
            pl.BlockSpec(memory_space=pl.ANY),
            pl.BlockSpec((1, k), lambda i, j: (0, 0)),
            pl.BlockSpec((k, tn), lambda i, j: (0, j)),
            pl.BlockSpec((1, tn), lambda i, j: (0, j)),
        ],
        out_specs=pl.BlockSpec((tn // LANES, tm, LANES), lambda i, j: (j, i, 0)),
        out_shape=jax.ShapeDtypeStruct((n // LANES, m, LANES), BF16),
        scratch_shapes=[pltpu.VMEM((tm, k), F32), pltpu.VMEM((tm, k), BF16), pltpu.SemaphoreType.DMA((1,))],
        compiler_params=_params("arbitrary", "arbitrary"),
        name="in_proj",
    )(x, g.reshape(1, k), w, b.reshape(1, n))


def _matmul_body(a_ref, w_ref, o_ref):
    o_ref[...] = jnp.dot(a_ref[...], w_ref[...], preferred_element_type=F32).astype(o_ref.dtype)


def _matmul(a, w, out_dtype):
    m, k = a.shape
    n = w.shape[1]
    tm = _tile(m, PROJ_ROWS)
    tn = _tile(n, KV_COLS)
    return pl.pallas_call(
        _matmul_body,
        grid=(m // tm, n // tn),
        in_specs=[pl.BlockSpec((tm, k), lambda i, j: (i, 0)), pl.BlockSpec((k, tn), lambda i, j: (0, j))],
        out_specs=pl.BlockSpec((tm, tn), lambda i, j: (i, j)),
        out_shape=jax.ShapeDtypeStruct((m, n), out_dtype),
        compiler_params=_params("parallel", "parallel"),
        name="matmul",
    )(a, w)


def _matmul_residual_body(a_ref, w_ref, r_ref, o_ref):
    o_ref[...] = r_ref[...] + jnp.dot(a_ref[...], w_ref[...], preferred_element_type=F32)


def _matmul_residual(a, w, res):
    m, k = a.shape
    n = w.shape[1]
    tm = _tile(m, PROJ_ROWS)
    tn = _tile(n, PROJ_COLS)
    return pl.pallas_call(
        _matmul_residual_body,
        grid=(m // tm, n // tn),
        in_specs=[
            pl.BlockSpec((tm, k), lambda i, j: (i, 0)),
            pl.BlockSpec((k, tn), lambda i, j: (0, j)),
            pl.BlockSpec((tm, tn), lambda i, j: (i, j)),
        ],
        out_specs=pl.BlockSpec((tm, tn), lambda i, j: (i, j)),
        out_shape=jax.ShapeDtypeStruct((m, n), F32),
        compiler_params=_params("parallel", "parallel"),
        name="out_proj",
    )(a, w, res)


def _fold_rows(x, op, rows=8):
    while x.shape[0] > rows:
        half = x.shape[0] // 2
        x = op(x[:half], x[half:])
    return x


def _moba_body(q_ref, k_ref, v_ref, town_ref, tprev_ref, bfar_ref, o_ref, km_ref, sel_ref, s_ref, l_ref, acc_ref,
               *, n_blocks, topk, heads):
    n = pl.program_id(2)
    blk = MOBA_BLOCK
    pair = 2 * blk
    scale = SCORE_SCALE
    group = range(heads)

    @pl.when(n == 0)
    def _():
        for g in group:
            for j in range(n_blocks):
                kj = k_ref[g, 0, j * blk:(j + 1) * blk, :].astype(F32)
                km_ref[g, j:j + 1, :] = jnp.mean(kj, axis=0, keepdims=True)

    q = [q_ref[g, 0] for g in group]

    for g in group:
        gate = lax.dot_general(km_ref[g].astype(BF16), q[g], _NT, preferred_element_type=F32)
        rows = lax.broadcasted_iota(jnp.int32, gate.shape, 0)
        gate = jnp.where(rows < n, gate, -jnp.inf)
        sel = jnp.zeros(gate.shape, jnp.bool_)
        for _ in range(topk):
            best = jnp.max(gate, axis=0, keepdims=True)
            first = jnp.min(jnp.where(gate == best, rows, n_blocks), axis=0, keepdims=True)
            pick = (rows == first) & (best > -jnp.inf)
            sel = sel | pick
            gate = jnp.where(pick, -jnp.inf, gate)
        sel_ref[g] = jnp.where(sel, 0.0, MASKED)

    def scores(g, rows):
        return lax.dot_general(k_ref[g, 0, rows, :], q[g], _NT, preferred_element_type=F32) * scale

    def pv(g, rows, p):
        return lax.dot_general(v_ref[g, 0, rows, :], p.astype(BF16), _TN, preferred_element_type=F32)

    jp = jnp.maximum(n - 1, 0)
    own_rows = pl.ds(pl.multiple_of(n * blk, blk), blk)
    prev_rows = pl.ds(pl.multiple_of(jp * blk, blk), blk)
    m8 = []
    for g in group:
        s_own = scores(g, own_rows) + town_ref[g]
        s_prev = scores(g, prev_rows) + tprev_ref[g] + sel_ref[g, pl.ds(jp, 1), :]
        s_ref[g, 0:blk, :] = s_own
        s_ref[g, blk:pair, :] = s_prev
        m8.append(jnp.maximum(_fold_rows(s_own, jnp.maximum), _fold_rows(s_prev, jnp.maximum)))

    n_far = n - 1
    n_pairs = lax.shift_right_logical(n, 1)

    def far_scores(c, m8):
        key_rows = pl.ds(pl.multiple_of(c * pair, pair), pair)
        base = pl.multiple_of(pair + c * pair, pair)
        j1 = 2 * c + 1
        out = []
        for g in group:
            bfar = bfar_ref[g]
            s0 = scores(g, pl.ds(pl.multiple_of(c * pair, blk), blk)) + (sel_ref[g, pl.ds(2 * c, 1), :] + bfar)
            s1 = scores(g, pl.ds(pl.multiple_of(c * pair + blk, blk), blk)) + jnp.where(
                j1 < n_far, sel_ref[g, pl.ds(j1, 1), :] + bfar, MASKED)
            s_ref[g, pl.ds(base, blk), :] = s0
            s_ref[g, pl.ds(base + blk, blk), :] = s1
            out.append(jnp.maximum(m8[g], jnp.maximum(_fold_rows(s0, jnp.maximum), _fold_rows(s1, jnp.maximum))))
        return tuple(out)

    m8 = lax.fori_loop(0, n_pairs, far_scores, tuple(m8))
    m = [jnp.max(m8[g], axis=0, keepdims=True) for g in group]

    for g in group:
        p_own = jnp.exp2(s_ref[g, 0:blk, :] - m[g])
        p_prev = jnp.exp2(s_ref[g, blk:pair, :] - m[g])
        l_ref[g] = _fold_rows(p_own, jnp.add) + _fold_rows(p_prev, jnp.add)
        acc_ref[g] = pv(g, own_rows, p_own) + pv(g, prev_rows, p_prev)

    def far_pv(c, carry):
        key_rows = pl.ds(pl.multiple_of(c * pair, pair), pair)
        base = pl.multiple_of(pair + c * pair, pair)
        for g in group:
            part = 0.0
            for half in range(2):
                rows = pl.ds(pl.multiple_of(c * pair + half * blk, blk), blk)
                p = jnp.exp2(s_ref[g, pl.ds(base + half * blk, blk), :] - m[g])
                l_ref[g] += _fold_rows(p, jnp.add)
                part = part + pv(g, rows, p)
            acc_ref[g] += part
        return carry

    lax.fori_loop(0, n_pairs, far_pv, 0)
    for g in group:
        l = jnp.sum(l_ref[g], axis=0, keepdims=True)
        o_ref[g, 0] = (acc_ref[g] / l).T.astype(o_ref.dtype)


def _moba(proj, town, tprev, bfar):
    _, b, s, _ = proj.shape
    blk = MOBA_BLOCK
    n_blocks = s // blk
    heads = MOBA_HEADS_PER_STEP
    body = functools.partial(_moba_body, n_blocks=n_blocks, topk=min(MOBA_TOPK, n_blocks), heads=heads)
    return pl.pallas_call(
        body,
        grid=(b, N_MOBA_HEADS // heads, n_blocks),
        in_specs=[
            pl.BlockSpec((heads, 1, blk, HEAD_DIM), lambda bi, h, n: (SLAB_QM // heads + h, bi, n, 0)),
            pl.BlockSpec((heads, 1, s, HEAD_DIM), lambda bi, h, n: (SLAB_KM // heads + h, bi, 0, 0)),
            pl.BlockSpec((heads, 1, s, HEAD_DIM), lambda bi, h, n: (SLAB_VM // heads + h, bi, 0, 0)),
            pl.BlockSpec((heads, blk, blk), lambda bi, h, n: (h, 0, 0)),
            pl.BlockSpec((heads, blk, blk), lambda bi, h, n: (h, 0, 0)),
            pl.BlockSpec((heads, 1, blk), lambda bi, h, n: (h, 0, 0)),
        ],
        out_specs=pl.BlockSpec((heads, 1, blk, HEAD_DIM), lambda bi, h, n: (h, bi, n, 0)),
        out_shape=jax.ShapeDtypeStruct((N_MOBA_HEADS, b, s, HEAD_DIM), F32),
        scratch_shapes=[
            pltpu.VMEM((heads, n_blocks, HEAD_DIM), F32),
            pltpu.VMEM((heads, n_blocks, blk), F32),
            pltpu.VMEM((heads, 2 * blk * (1 + (n_blocks - 1) // 2), blk), F32),
            pltpu.VMEM((heads, 8, blk), F32),
            pltpu.VMEM((heads, HEAD_DIM, blk), F32),
        ],
        compiler_params=_params("parallel", "parallel", "arbitrary"),
        name="moba",
    )(proj, proj, proj, town, tprev, bfar)


def _swa_body(q_ref, k_ref, v_ref, bias_ref, sink_ref, o_ref, *, blocks_per_step):
    blk = SWA_BLOCK
    scale = SCORE_SCALE
    sink = sink_ref[0]
    blocks = range(blocks_per_step)

    def band(ref, r):
        n = pl.program_id(2) * blocks_per_step + r
        prev_rows = pl.ds(pl.multiple_of(jnp.maximum(n - 1, 0) * blk, blk), blk)
        own_rows = pl.ds(pl.multiple_of(n * blk, blk), blk)
        return jnp.concatenate([ref[0, 0, prev_rows, :], ref[0, 0, own_rows, :]], axis=0)

    raw = []
    for r in blocks:
        q = jnp.concatenate([q_ref[g, 0, r * blk:(r + 1) * blk, :] for g in range(SWA_GROUP)], axis=0)
        raw.append(lax.dot_general(band(k_ref, r), q, _NT, preferred_element_type=F32))
    soft = []
    for r in blocks:
        n = pl.program_id(2) * blocks_per_step + r
        s = raw[r] * scale + bias_ref[0]
        key_row = lax.broadcasted_iota(jnp.int32, s.shape, 0)
        s = jnp.where((key_row < blk) & (n == 0), MASKED, s)
        m = jnp.maximum(jnp.max(s, axis=0, keepdims=True), sink)
        p = jnp.exp2(s - m)
        soft.append((p.astype(BF16), jnp.sum(p, axis=0, keepdims=True) + jnp.exp2(sink - m)))
    for r in blocks:
        p, l = soft[r]
        o = lax.dot_general(band(v_ref, r), p, _TN, preferred_element_type=F32) / l
        for g in range(SWA_GROUP):
            o_ref[g, 0, r * blk:(r + 1) * blk, :] = o[:, g * blk:(g + 1) * blk].T.astype(o_ref.dtype)


def _swa(proj, bias, sink):
    _, b, s, _ = proj.shape
    blk = SWA_BLOCK
    gq = SWA_GROUP * blk
    per_step = SWA_BLOCKS_PER_STEP if (s // blk) % SWA_BLOCKS_PER_STEP == 0 else 1
    rows = per_step * blk
    return pl.pallas_call(
        functools.partial(_swa_body, blocks_per_step=per_step),
        grid=(b, N_SWA_KV_HEADS, s // rows),
        in_specs=[
            pl.BlockSpec((SWA_GROUP, 1, rows, HEAD_DIM), lambda bi, h, n: (SLAB_QS // SWA_GROUP + h, bi, n, 0)),
            pl.BlockSpec((1, 1, s, HEAD_DIM), lambda bi, h, n: (SLAB_KS + h, bi, 0, 0)),
            pl.BlockSpec((1, 1, s, HEAD_DIM), lambda bi, h, n: (SLAB_VS + h, bi, 0, 0)),
            pl.BlockSpec((1, 2 * blk, gq), lambda bi, h, n: (h, 0, 0)),
            pl.BlockSpec((1, 1, gq), lambda bi, h, n: (h, 0, 0)),
        ],
        out_specs=pl.BlockSpec((SWA_GROUP, 1, rows, HEAD_DIM), lambda bi, h, n: (h, bi, n, 0)),
        out_shape=jax.ShapeDtypeStruct((N_SWA_HEADS, b, s, HEAD_DIM), F32),
        compiler_params=_params("parallel", "parallel", "arbitrary"),
        name="swa",
    )(proj, proj, proj, bias, sink)


def _head_norm_body(om_ref, os_ref, gm_ref, gs_ref, o_ref):
    col = 0
    for x_ref, g_ref in ((om_ref, gm_ref), (os_ref, gs_ref)):
        heads = x_ref.shape[0]
        ss = jnp.sum(sum(jnp.square(x_ref[c]) for c in range(heads)), axis=-1, keepdims=True)
        r = lax.rsqrt(ss / (heads * HEAD_DIM) + EPS)
        for c in range(heads):
            o_ref[:, col:col + HEAD_DIM] = (x_ref[c] * r * g_ref[:, c * HEAD_DIM:(c + 1) * HEAD_DIM]).astype(o_ref.dtype)
            col += HEAD_DIM


def _head_norm(om, osw, gm, gs):
    hm, m, _ = om.shape
    hs = osw.shape[0]
    tm = _tile(m, HEAD_NORM_ROWS)
    width = (hm + hs) * HEAD_DIM
    return pl.pallas_call(
        _head_norm_body,
        grid=(m // tm,),
        in_specs=[
            pl.BlockSpec((hm, tm, HEAD_DIM), lambda i: (0, i, 0)),
            pl.BlockSpec((hs, tm, HEAD_DIM), lambda i: (0, i, 0)),
            pl.BlockSpec((1, hm * HEAD_DIM), lambda i: (0, 0)),
            pl.BlockSpec((1, hs * HEAD_DIM), lambda i: (0, 0)),
        ],
        out_specs=pl.BlockSpec((tm, width), lambda i: (i, 0)),
        out_shape=jax.ShapeDtypeStruct((m, width), BF16),
        compiler_params=_params("parallel"),
        name="head_norm",
    )(om, osw, gm.reshape(1, -1), gs.reshape(1, -1))


def _cross_body(x_ref, g_ref, wq_ref, kv_ref, wo_ref, o_ref, h_ref):
    scale = SCORE_SCALE
    tm = x_ref.shape[0]

    def norm_rows(r, carry):
        rows = pl.ds(pl.multiple_of(r * FFN_ROW_CHUNK, FFN_ROW_CHUNK), FFN_ROW_CHUNK)
        h_ref[rows, :] = _rms_rows(x_ref[rows, :], g_ref[...]).astype(BF16)
        return carry

    lax.fori_loop(0, tm // FFN_ROW_CHUNK, norm_rows, 0)
    q = jnp.dot(h_ref[...], wq_ref[...], preferred_element_type=F32).astype(BF16)
    outs = []
    for hd in range(N_CROSS_HEADS):
        cols = slice(hd * HEAD_DIM, (hd + 1) * HEAD_DIM)
        k = kv_ref[0, :, cols]
        v = kv_ref[0, :, CROSS_W + hd * HEAD_DIM:CROSS_W + (hd + 1) * HEAD_DIM]
        s = lax.dot_general(q[:, cols], k, _NT, preferred_element_type=F32) * scale
        p = jnp.exp2(s - jnp.max(s, axis=-1, keepdims=True))
        l = jnp.sum(p, axis=-1, keepdims=True)
        outs.append((jnp.dot(p.astype(BF16), v, preferred_element_type=F32) / l).astype(BF16))
    o = jnp.concatenate(outs, axis=-1)
    o_ref[...] = x_ref[...] + jnp.dot(o, wo_ref[...], preferred_element_type=F32)


def _cross(x, g, wq, kv, wo, seq):
    m, d = x.shape
    n_mem = kv.shape[1]
    tm = _tile(seq, CROSS_ROWS)
    assert tm % FFN_ROW_CHUNK == 0
    tiles_per_seq = seq // tm
    resident = pl.Buffered(1)
    return pl.pallas_call(
        _cross_body,
        grid=(m // tm,),
        in_specs=[
            pl.BlockSpec((tm, d), lambda i: (i, 0)),
            pl.BlockSpec((1, d), lambda i: (0, 0)),
            pl.BlockSpec((d, CROSS_W), lambda i: (0, 0), pipeline_mode=resident),
            pl.BlockSpec((1, n_mem, 2 * CROSS_W), lambda i: (i // tiles_per_seq, 0, 0)),
            pl.BlockSpec((CROSS_W, d), lambda i: (0, 0), pipeline_mode=resident),
        ],
        out_specs=pl.BlockSpec((tm, d), lambda i: (i, 0)),
        out_shape=jax.ShapeDtypeStruct((m, d), F32),
        scratch_shapes=[pltpu.VMEM((tm, d), BF16)],
        compiler_params=_params("parallel"),
        name="cross",
    )(x, g.reshape(1, d), wq, kv, wo)


def _t5_bucket(dist):
    n = np.maximum(dist, 0)
    max_exact = NUM_BUCKETS // 2
    nf = np.maximum(n, max_exact).astype(np.float32)
    large = max_exact + (np.log(nf / np.float32(max_exact)) / np.float32(math.log(MAX_DISTANCE / max_exact))
                         * np.float32(NUM_BUCKETS - max_exact)).astype(np.int32)
    large = np.minimum(large, NUM_BUCKETS - 1)
    return np.where(n < max_exact, n, large).astype(np.int32)


def _bias_by_distance(bias_hb, dists):
    onehot = _t5_bucket(dists)[None, :] == np.arange(NUM_BUCKETS)[:, None]
    return jnp.sum(jnp.where(onehot[None], bias_hb[:, :, None], 0.0), axis=1)


def _toeplitz(g, rows, cols):
    heads, n = g.shape
    assert n == rows + cols - 1
    width = rows + cols
    tiled = jnp.broadcast_to(jnp.pad(g, ((0, 0), (0, 1)))[:, None, :], (heads, rows, width))
    skew = tiled.reshape(heads, rows * width)[:, :rows * (width - 1)].reshape(heads, rows, width - 1)
    return skew[:, :, rows - 1:rows - 1 + cols]


def _moba_bias_tables(bias_hb):
    blk = MOBA_BLOCK
    d_own = np.arange(2 * blk - 1) - (blk - 1)
    g_own = jnp.where(d_own >= 0, _bias_by_distance(bias_hb, np.maximum(d_own, 0)), MASKED)
    g_prev = _bias_by_distance(bias_hb, d_own + blk)
    far = int(_t5_bucket(np.array(blk + 1)))
    bfar = jnp.broadcast_to(bias_hb[:, far][:, None, None], (bias_hb.shape[0], 1, blk))
    return _toeplitz(g_own, blk, blk), _toeplitz(g_prev, blk, blk), bfar


def _swa_bias_table(bias_hb):
    blk = SWA_BLOCK
    dist = np.arange(3 * blk - 1) - (blk - 1)
    ok = (dist >= 0) & (dist < SWA_WINDOW)
    g = jnp.where(ok, _bias_by_distance(bias_hb, np.maximum(dist, 0)), MASKED)
    t = _toeplitz(g, 2 * blk, blk)
    t = t.reshape(N_SWA_KV_HEADS, SWA_GROUP, 2 * blk, blk).transpose(0, 2, 1, 3)
    return t.reshape(N_SWA_KV_HEADS, 2 * blk, SWA_GROUP * blk)


def kernel(x, mem, rel_bias, g_final, g_ffn1, w1_gate, w1_up, w1_down, g_mix, w_in, b_in, sinks, g_out_moba, g_out_swa, w_out, g_cross, g_mem, w_cq, w_ck, w_cv, w_co, g_ffn2, w2_gate, w2_up, w2_down):
    b, s, d = x.shape
    n_mem = mem.shape[1]
    depth = w_in.shape[0]
    assert s % MOBA_BLOCK == 0 and s % SWA_BLOCK == 0 and d % LANES == 0

    bias_moba = rel_bias[:, :N_MOBA_HEADS].T.astype(F32)
    bias_swa = rel_bias[:, N_MOBA_HEADS:].T.astype(F32)
    town, tprev, bfar = _moba_bias_tables(bias_moba * LOG2E)
    tswa = _swa_bias_table(bias_swa * LOG2E)

    xt = x.reshape(b * s, d)
    memt = mem.reshape(b * n_mem, d)
    for l in range(depth):
        w1 = [w.astype(BF16) for w in (w1_gate[l], w1_up[l], w1_down[l])]
        w2 = (w2_gate[l], w2_up[l], w2_down[l])
        if _can_side_cast(b * s, d, w1[0].shape[1]) and all(p.shape == q.shape for p, q in zip(w1, w2)):
            xt, *w2 = _ffn(xt, g_ffn1[l], *w1, cast=w2)
        else:
            xt = _ffn(xt, g_ffn1[l], *w1)
            w2 = [w.astype(BF16) for w in w2]

        proj = _in_proj(xt, g_mix[l], w_in[l].astype(BF16), b_in[l]).reshape(N_SLABS, b, s, HEAD_DIM)
        o_m = _moba(proj, town, tprev, bfar).reshape(N_MOBA_HEADS, b * s, HEAD_DIM)
        sink = jnp.broadcast_to((sinks[l].astype(F32) * LOG2E).reshape(N_SWA_KV_HEADS, SWA_GROUP, 1),
                                (N_SWA_KV_HEADS, SWA_GROUP, SWA_BLOCK)).reshape(N_SWA_KV_HEADS, 1, SWA_GROUP * SWA_BLOCK)
        o_s = _swa(proj, tswa, sink).reshape(N_SWA_HEADS, b * s, HEAD_DIM)
        o = _head_norm(o_m, o_s, g_out_moba[l], g_out_swa[l])
        xt = _matmul_residual(o, w_out[l].astype(BF16), xt)

        mem_n = _rmsnorm(memt, g_mem[l], BF16)
        w_kv = jnp.concatenate([w_ck[l], w_cv[l]], axis=1).astype(BF16)
        kv = _matmul(mem_n, w_kv, BF16).reshape(b, n_mem, 2 * CROSS_W)
        xt = _cross(xt, g_cross[l], w_cq[l].astype(BF16), kv, w_co[l].astype(BF16), s)

        xt = _ffn(xt, g_ffn2[l], *w2, g_out=g_final if l == depth - 1 else None)
    return xt.reshape(b, s, d)
```
